```python
import math
import jax, jax.numpy as jnp
from jax import lax
import numpy as np

D_MODEL = 1024
BATCH = 8
SEQ = 4096
DEPTH = 1

D_MIX = D_MODEL
MOBA_HEADS = 8
MOBA_HEAD_DIM = 64
MOBA_WIDTH = MOBA_HEADS * MOBA_HEAD_DIM
MOBA_BLOCK = 256
MOBA_TOPK = 3
MOBA_QCHUNK = 128
ROPE_THETA = 500000.0
ROPE_DIM = MOBA_HEAD_DIM // 4
DN_HEADS = 4
DN_KEY_DIM = 128
DN_VAL_DIM = 128
DN_QK_WIDTH = DN_HEADS * DN_KEY_DIM
DN_WIDTH = DN_HEADS * DN_VAL_DIM
DN_CONV = 4
DN_CONV_CH = 2 * DN_QK_WIDTH + DN_WIDTH
DN_CHUNK = 64
IN_SPLITS = (MOBA_WIDTH, MOBA_WIDTH, MOBA_WIDTH, MOBA_WIDTH,
             DN_QK_WIDTH, DN_QK_WIDTH, DN_WIDTH, DN_WIDTH, DN_HEADS, DN_HEADS)
D_IN = sum(IN_SPLITS)
NORM_EPS = 1e-6
NEG_INF = -1e30

kernel_name = "hymba_moba_gated_deltanet_layer"


def rmsnorm(x, w):
    xf = x.astype(jnp.float32)
    y = xf * lax.rsqrt(jnp.mean(xf * xf, axis=-1, keepdims=True) + NORM_EPS)
    return (y * w.astype(jnp.float32)).astype(x.dtype)


def l2norm(x):
    return x * lax.rsqrt(jnp.sum(x * x, axis=-1, keepdims=True) + NORM_EPS)


def partial_rope(x, pos):
    half = ROPE_DIM // 2
    inv_freq = jnp.power(ROPE_THETA, -jnp.arange(0, ROPE_DIM, 2, dtype=jnp.float32) / ROPE_DIM)
    ang = pos.astype(jnp.float32)[:, None] * inv_freq[None, :]
    cos = jnp.cos(ang)[None, :, None, :]
    sin = jnp.sin(ang)[None, :, None, :]
    xr = x[..., :ROPE_DIM].astype(jnp.float32)
    x1, x2 = xr[..., :half], xr[..., half:]
    rot = jnp.concatenate([x1 * cos - x2 * sin, x2 * cos + x1 * sin], axis=-1).astype(x.dtype)
    return jnp.concatenate([rot, x[..., ROPE_DIM:]], axis=-1)


def moba_attention(q, k, v):
    B, S, H, d = q.shape
    nb = -(-S // MOBA_BLOCK)
    pad = nb * MOBA_BLOCK - S
    nq = S // MOBA_QCHUNK
    topk = min(MOBA_TOPK, nb)
    scale = d ** -0.5
    qh = q.transpose(0, 2, 1, 3)
    kh = jnp.pad(k.transpose(0, 2, 1, 3), ((0, 0), (0, 0), (0, pad), (0, 0)))
    vh = jnp.pad(v.transpose(0, 2, 1, 3), ((0, 0), (0, 0), (0, pad), (0, 0)))
    kb = kh.reshape(B, H, nb, MOBA_BLOCK, d)
    vb = vh.reshape(B, H, nb, MOBA_BLOCK, d)
    k_mean = jnp.mean(kb.astype(jnp.float32), axis=3)
    blk_ids = jnp.arange(nb)
    h_ids = jnp.arange(H)[:, None, None]

    def one_chunk(i):
        b = i // nq
        start = (i % nq) * MOBA_QCHUNK
        qc = lax.dynamic_slice_in_dim(lax.dynamic_index_in_dim(qh, b, 0, keepdims=False),
                                      start, MOBA_QCHUNK, axis=1)
        kb_b = lax.dynamic_index_in_dim(kb, b, 0, keepdims=False)
        vb_b = lax.dynamic_index_in_dim(vb, b, 0, keepdims=False)
        km_b = lax.dynamic_index_in_dim(k_mean, b, 0, keepdims=False)
        own = start // MOBA_BLOCK
        q_pos = start + jnp.arange(MOBA_QCHUNK)
        gate = jnp.einsum('hqd,hnd->hqn', qc.astype(jnp.float32), km_b)
        gate = jnp.where(blk_ids < own, gate, NEG_INF)
        _, sel = lax.top_k(gate, topk)
        sel_ok = sel < own
        k_sel = kb_b[h_ids, sel]
        v_sel = vb_b[h_ids, sel]
        s_sel = jnp.einsum('hqd,hqnkd->hqnk', qc, k_sel).astype(jnp.float32) * scale
        s_sel = jnp.where(sel_ok[..., None], s_sel, NEG_INF).reshape(H, MOBA_QCHUNK, topk * MOBA_BLOCK)
        k_own = lax.dynamic_index_in_dim(kb_b, own, 1, keepdims=False)
        v_own = lax.dynamic_index_in_dim(vb_b, own, 1, keepdims=False)
        k_pos = own * MOBA_BLOCK + jnp.arange(MOBA_BLOCK)
        s_own = jnp.einsum('hqd,hkd->hqk', qc, k_own).astype(jnp.float32) * scale
        s_own = jnp.where(k_pos[None, None, :] <= q_pos[None, :, None], s_own, NEG_INF)
        p = jax.nn.softmax(jnp.concatenate([s_sel, s_own], axis=-1), axis=-1).astype(v.dtype)
        p_sel = p[..., :topk * MOBA_BLOCK].reshape(H, MOBA_QCHUNK, topk, MOBA_BLOCK)
        p_own = p[..., topk * MOBA_BLOCK:]
        return (jnp.einsum('hqnk,hqnkd->hqd', p_sel, v_sel)
                + jnp.einsum('hqk,hkd->hqd', p_own, v_own))

    o = lax.map(one_chunk, jnp.arange(B * nq))
    return o.reshape(B, nq, H, MOBA_QCHUNK, d).transpose(0, 1, 3, 2, 4).reshape(B, S, H, d)


def short_conv(x, w):
    y = lax.conv_general_dilated(x, w[:, None, :].astype(x.dtype), window_strides=(1,),
                                 padding=((DN_CONV - 1, 0),),
                                 dimension_numbers=('NWC', 'WIO', 'NWC'),
                                 feature_group_count=x.shape[-1])
    return jax.nn.silu(y)


def gated_delta_rule(q, k, v, g, beta):
    B, S, H, dk = q.shape
    dv = v.shape[-1]
    out_dtype = v.dtype
    N, C = S // DN_CHUNK, DN_CHUNK
    f32 = jnp.float32
    q = l2norm(q.astype(f32)) * (dk ** -0.5)
    k = l2norm(k.astype(f32))
    v = v.astype(f32)
    to_chunks = lambda t: t.reshape(B, N, C, H, -1).transpose(0, 3, 1, 2, 4)
    q, k, v = to_chunks(q), to_chunks(k), to_chunks(v)
    g = g.astype(f32).reshape(B, N, C, H).transpose(0, 3, 1, 2)
    beta = beta.astype(f32).reshape(B, N, C, H).transpose(0, 3, 1, 2)
    G = jnp.cumsum(g, axis=-1)
    tri = jnp.tril(jnp.ones((C, C), dtype=bool))
    tri_strict = jnp.tril(jnp.ones((C, C), dtype=bool), -1)
    diff = G[..., :, None] - G[..., None, :]
    decay = jnp.where(tri, jnp.exp(jnp.where(tri, diff, 0.0)), 0.0)
    decay_strict = jnp.where(tri_strict, decay, 0.0)
    kb = k * beta[..., None]
    A = jnp.einsum('bhnid,bhnjd->bhnij', kb, k) * decay_strict
    sys = A + jnp.eye(C, dtype=f32)
    u = lax.linalg.triangular_solve(sys, v * beta[..., None], left_side=True, lower=True, unit_diagonal=True)
    w = lax.linalg.triangular_solve(sys, kb * jnp.exp(G)[..., None], left_side=True, lower=True, unit_diagonal=True)
    qk = jnp.einsum('bhnid,bhnjd->bhnij', q, k) * decay
    q_dec = q * jnp.exp(G)[..., None]
    k_dec = k * jnp.exp(G[..., -1:] - G)[..., None]
    g_last = jnp.exp(G[..., -1])

    def step(state, xs):
        q_c, k_c, u_c, w_c, qk_c, gl = xs
        v_new = u_c - jnp.einsum('bhcd,bhde->bhce', w_c, state)
        o = jnp.einsum('bhcd,bhde->bhce', q_c, state) + jnp.einsum('bhij,bhje->bhie', qk_c, v_new)
        state = state * gl[..., None, None] + jnp.einsum('bhcd,bhce->bhde', k_c, v_new)
        return state, o

    xs = tuple(jnp.moveaxis(t, 2, 0) for t in (q_dec, k_dec, u, w, qk, g_last))
    s0 = jnp.zeros((B, H, dk, dv), f32)
    _, o = lax.scan(step, s0, xs)
    return o.transpose(1, 0, 3, 2, 4).reshape(B, S, H, dv).astype(out_dtype)


def setup_inputs(seed: int = 0) -> dict:
    key = jax.random.key(seed)
    ks = jax.random.split(key, 12)
    f32 = jnp.float32
    nrm = lambda k_, shape: jax.random.normal(k_, shape, f32)
    x = nrm(ks[0], (BATCH, SEQ, D_MODEL))
    c = nrm(ks[1], (BATCH, D_MODEL))
    ada_w = nrm(ks[2], (DEPTH, D_MODEL, 3 * D_MODEL)) * (0.5 * D_MODEL ** -0.5)
    ada_b = 0.01 * nrm(ks[3], (DEPTH, 3 * D_MODEL))
    pre_norm_w = 1.0 + 0.02 * nrm(ks[4], (DEPTH, D_MODEL))
    post_norm_w = 1.0 + 0.02 * nrm(ks[5], (DEPTH, D_MODEL))
    w_in = nrm(ks[6], (DEPTH, D_MODEL, D_IN)) * D_MODEL ** -0.5
    conv_w = nrm(ks[7], (DEPTH, DN_CONV, DN_CONV_CH)) * DN_CONV ** -0.5
    a_log = jnp.log(jax.random.uniform(ks[8], (DEPTH, DN_HEADS), f32, 1.0, 16.0))
    dt = jnp.exp(jax.random.uniform(ks[9], (DEPTH, DN_HEADS), f32, math.log(0.001), math.log(0.1)))
    dt_bias = dt + jnp.log(-jnp.expm1(-dt))
    dn_norm_w = 1.0 + 0.02 * nrm(ks[10], (DEPTH, DN_VAL_DIM))
    w_out = nrm(ks[11], (DEPTH, D_MIX, D_MODEL)) * D_MIX ** -0.5
    return {"x": x, "c": c, "ada_w": ada_w, "ada_b": ada_b, "pre_norm_w": pre_norm_w,
            "post_norm_w": post_norm_w, "w_in": w_in, "conv_w": conv_w, "a_log": a_log,
            "dt_bias": dt_bias, "dn_norm_w": dn_norm_w, "w_out": w_out}


def reference(x, c, ada_w, ada_b, pre_norm_w, post_norm_w, w_in, conv_w, a_log, dt_bias, dn_norm_w, w_out):
    B, S, _ = x.shape
    pos = jnp.arange(S, dtype=jnp.int32)
    split_pts = [int(p) for p in np.cumsum(IN_SPLITS)[:-1]]
    for l in range(DEPTH):
        mod = jax.nn.silu(c) @ ada_w[l] + ada_b[l]
        shift, scale, gate = jnp.split(mod, 3, axis=-1)
        h = rmsnorm(x, pre_norm_w[l]) * (1.0 + scale[:, None, :]) + shift[:, None, :]
        proj = h @ w_in[l]
        mq, mk, mv, mg, dq, dk, dv, dz, db, da = jnp.split(proj, split_pts, axis=-1)
        mq = partial_rope(mq.reshape(B, S, MOBA_HEADS, MOBA_HEAD_DIM), pos)
        mk = partial_rope(mk.reshape(B, S, MOBA_HEADS, MOBA_HEAD_DIM), pos)
        mv = mv.reshape(B, S, MOBA_HEADS, MOBA_HEAD_DIM)
        o_moba = moba_attention(mq, mk, mv).reshape(B, S, MOBA_WIDTH) * jax.nn.silu(mg)
        qkv = short_conv(jnp.concatenate([dq, dk, dv], axis=-1), conv_w[l])
        dq, dk, dv = jnp.split(qkv, [DN_QK_WIDTH, 2 * DN_QK_WIDTH], axis=-1)
        g = -jnp.exp(a_log[l].astype(jnp.float32)) * jax.nn.softplus(
            da.astype(jnp.float32) + dt_bias[l].astype(jnp.float32))
        beta = jax.nn.sigmoid(db.astype(jnp.float32))
        o_dn = gated_delta_rule(dq.reshape(B, S, DN_HEADS, DN_KEY_DIM),
                                dk.reshape(B, S, DN_HEADS, DN_KEY_DIM),
                                dv.reshape(B, S, DN_HEADS, DN_VAL_DIM), g, beta)
        o_dn = rmsnorm(o_dn, dn_norm_w[l]).reshape(B, S, DN_WIDTH) * jax.nn.silu(dz)
        y = jnp.concatenate([o_moba, o_dn], axis=-1) @ w_out[l]
        x = x + gate[:, None, :] * rmsnorm(y, post_norm_w[l])
    return x
```

```python
import functools
import math

import jax
import jax.numpy as jnp
from jax import lax
from jax.experimental import pallas as pl
from jax.experimental.pallas import tpu as pltpu

F32 = jnp.float32
BF16 = jnp.bfloat16
HI = lax.Precision.HIGHEST

D = 1024
MH, MD = 8, 64
MW = MH * MD
BLK = 256
TOPK = 3
ROPE_D = 16
ROPE_THETA = 500000.0
DH, DK = 4, 128
DW = DH * DK
CONV = 4
CH = 64
EPS = 1e-6
NEG = -1e30
LOWEST = -3.0e38

LANES = 128
VMEM_LIMIT = 56 * 1024 * 1024

TM = 256
TD = 512
TO = 512

NT = (((1,), (1,)), ((), ()))
TN = (((0,), (0,)), ((), ()))


def _silu(v):
    return v * jax.nn.sigmoid(v)


def _mod_kernel(c_ref, w_ref, b_ref, o_ref):
    c = c_ref[...]
    o_ref[...] = jnp.dot(_silu(c), w_ref[...], precision=HI, preferred_element_type=F32) + b_ref[...]


def _mod(c, ada_w, ada_b):
    B = c.shape[0]
    tn = 512
    return pl.pallas_call(
        _mod_kernel,
        grid=(3 * D // tn,),
        in_specs=[pl.BlockSpec((B, D), lambda j: (0, 0)),
                  pl.BlockSpec((D, tn), lambda j: (0, j)),
                  pl.BlockSpec((1, tn), lambda j: (0, j))],
        out_specs=pl.BlockSpec((B, tn), lambda j: (0, j)),
        out_shape=jax.ShapeDtypeStruct((B, 3 * D), F32),
        name="mod",
    )(c, ada_w, ada_b.reshape(1, 3 * D))


def _inproj_kernel(x_ref, mod_ref, pw_ref, wqk_ref, wvt_ref, wg_ref, wd_ref, wz_ref, ws_ref,
                   rc_ref, rs1_ref, rs2_ref, alog_ref, dtb_ref,
                   q_ref, k_ref, km_ref, vt_ref, sg_ref, dx_ref, sz_ref, gb_ref, gbt_ref):
    x = x_ref[0]
    shift = mod_ref[0, 0:1, :]
    scale = mod_ref[0, 1:2, :]
    y = x * lax.rsqrt(jnp.mean(x * x, axis=-1, keepdims=True) + EPS) * pw_ref[...]
    h = (y * (1.0 + scale) + shift).astype(BF16)

    rc, rs1, rs2 = rc_ref[...], rs1_ref[...], rs2_ref[...]

    def rope(t):
        tm1 = pltpu.roll(t, MW - ROPE_D // 2, axis=1)
        tp1 = pltpu.roll(t, ROPE_D // 2, axis=1)
        parts = []
        for n in range(MW // LANES):
            sl = slice(n * LANES, (n + 1) * LANES)
            parts.append(t[:, sl] * rc + tm1[:, sl] * rs1 + tp1[:, sl] * rs2)
        return jnp.concatenate(parts, axis=1)

    qk = jnp.dot(h, wqk_ref[...], preferred_element_type=F32)
    q_ref[0] = rope(qk[:, :MW])
    k = rope(qk[:, MW:])
    k_ref[0] = k.astype(BF16)
    km_ref[0, 0] = jnp.mean(k, axis=0, keepdims=True)

    vt = lax.dot_general(wvt_ref[...], h, NT, preferred_element_type=F32)
    vt_ref[0, 0] = vt.astype(BF16)

    sg_ref[0] = _silu(jnp.dot(h, wg_ref[...], preferred_element_type=F32))
    dx_ref[0] = jnp.dot(h, wd_ref[...], preferred_element_type=F32)
    sz_ref[0] = _silu(jnp.dot(h, wz_ref[...], preferred_element_type=F32))

    s = jnp.dot(h, ws_ref[...], preferred_element_type=F32)
    lane = lax.broadcasted_iota(jnp.int32, s.shape, 1)
    beta = jax.nn.sigmoid(s)
    z = s + dtb_ref[...]
    softplus = jnp.maximum(z, 0.0) + jnp.log1p(jnp.exp(-jnp.abs(z)))
    g = -jnp.exp(alog_ref[...]) * softplus
    gb = jnp.where(lane < DH, beta, jnp.where(lane < 2 * DH, g, 0.0))
    gb_ref[0] = gb
    gbt = gb.T[0:8, :]
    for cidx in range(TM // CH):
        gbt_ref[0, cidx] = gbt[:, cidx * CH:(cidx + 1) * CH]


def _inproj(x, mod3, pre_w, w_in, a_log, dt_bias):
    B, S, _ = x.shape
    nb = S // BLK
    nt = S // TM
    wb = w_in.astype(BF16)
    wqk = wb[:, 0:2 * MW]
    wvt = wb[:, 2 * MW:3 * MW].T
    wg = wb[:, 3 * MW:4 * MW]
    wd = wb[:, 4 * MW:4 * MW + 3 * DW]
    wz = wb[:, 4 * MW + 3 * DW:4 * MW + 4 * DW]
    ws = jnp.pad(wb[:, 4 * MW + 4 * DW:], ((0, 0), (0, LANES - 2 * DH)))
    alog = jnp.pad(a_log.astype(F32), (DH, LANES - 2 * DH)).reshape(1, LANES)
    dtb = jnp.pad(dt_bias.astype(F32), (DH, LANES - 2 * DH)).reshape(1, LANES)

    half = ROPE_D // 2
    inv_freq = jnp.power(ROPE_THETA, -jnp.arange(0, ROPE_D, 2, dtype=F32) / ROPE_D)
    ang = jnp.arange(S, dtype=jnp.int32).astype(F32)[:, None] * inv_freq[None, :]
    cos, sin = jnp.cos(ang), jnp.sin(ang)
    one = jnp.ones((S, MD - ROPE_D), F32)
    zero = jnp.zeros((S, MD - ROPE_D), F32)
    zh = jnp.zeros((S, half), F32)
    rc = jnp.tile(jnp.concatenate([cos, cos, one], axis=1), (1, 2))
    rs1 = jnp.tile(jnp.concatenate([-sin, zh, zero], axis=1), (1, 2))
    rs2 = jnp.tile(jnp.concatenate([zh, sin, zero], axis=1), (1, 2))

    const = lambda shape: pl.BlockSpec(shape, lambda b, t: (0,) * len(shape))
    row = lambda w: pl.BlockSpec((1, TM, w), lambda b, t: (b, t, 0))
    tab = pl.BlockSpec((TM, LANES), lambda b, t: (t, 0))
    out_shapes = [
        jax.ShapeDtypeStruct((B, S, MW), F32),
        jax.ShapeDtypeStruct((B, S, MW), BF16),
        jax.ShapeDtypeStruct((B, nb, 1, MW), F32),
        jax.ShapeDtypeStruct((B, nb, MW, BLK), BF16),
        jax.ShapeDtypeStruct((B, S, MW), F32),
        jax.ShapeDtypeStruct((B, S, 3 * DW), F32),
        jax.ShapeDtypeStruct((B, S, DW), F32),
        jax.ShapeDtypeStruct((B, S, LANES), F32),
        jax.ShapeDtypeStruct((B, S // CH, 8, CH), F32),
    ]
    out_specs = [
        row(MW), row(MW),
        pl.BlockSpec((1, 1, 1, MW), lambda b, t: (b, t, 0, 0)),
        pl.BlockSpec((1, 1, MW, BLK), lambda b, t: (b, t, 0, 0)),
        row(MW), row(3 * DW), row(DW), row(LANES),
        pl.BlockSpec((1, TM // CH, 8, CH), lambda b, t: (b, t, 0, 0)),
    ]
    return pl.pallas_call(
        _inproj_kernel,
        grid=(B, nt),
        in_specs=[row(D), pl.BlockSpec((1, 3, D), lambda b, t: (b, 0, 0)), const((1, D)),
                  const((D, 2 * MW)), const((MW, D)), const((D, MW)), const((D, 3 * DW)),
                  const((D, DW)), const((D, LANES)), tab, tab, tab,
                  const((1, LANES)), const((1, LANES))],
        out_specs=out_specs,
        out_shape=out_shapes,
        compiler_params=pltpu.CompilerParams(
            dimension_semantics=("parallel", "parallel"), vmem_limit_bytes=VMEM_LIMIT),
        name="inproj",
    )(x, mod3, pre_w.reshape(1, D), wqk, wvt, wg, wd, wz, ws, rc, rs1, rs2, alog, dtb)


def _moba_kernel(q_ref, k_ref, vt_ref, km_ref, sg_ref, o_ref, bias_ref, *, nb):
    i = pl.program_id(2)
    q = q_ref[0]
    km = km_ref[0]
    lane = lax.broadcasted_iota(jnp.int32, (1, LANES), 1)
    head_lanes = [lane < MD, lane >= MD]
    blk = lax.broadcasted_iota(jnp.int32, (nb, BLK), 0)
    past = blk < i
    scale = MD ** -0.5

    qs = []
    for h in range(2):
        kmh = jnp.where(head_lanes[h], km, 0.0)
        gate = lax.dot_general(kmh, q, NT, precision=HI, preferred_element_type=F32)
        g = jnp.where(past, gate, NEG)
        sel = jnp.zeros((nb, BLK), jnp.bool_)
        for _ in range(min(TOPK, nb)):
            m = jnp.max(g, axis=0, keepdims=True)
            idx = jnp.min(jnp.where(g == m, blk, nb), axis=0, keepdims=True)
            pick = blk == idx
            sel = jnp.logical_or(sel, pick)
            g = jnp.where(pick, LOWEST, g)
        bias_ref[h] = jnp.where(jnp.logical_and(sel, past), 0.0, NEG)
        qs.append((jnp.where(head_lanes[h], q, 0.0) * scale).astype(BF16))

    def attend(kj, vtj, biases, carry):
        out = []
        for h in range(2):
            m, l, acc = carry[h]
            s = lax.dot_general(kj, qs[h], NT, preferred_element_type=F32) + biases[h]
            mn = jnp.maximum(m, jnp.max(s, axis=0, keepdims=True))
            alpha = jnp.exp(m - mn)
            p = jnp.exp(s - mn)
            l = alpha * l + jnp.sum(p, axis=0, keepdims=True)
            pv = jnp.dot(vtj[h * MD:(h + 1) * MD, :], p.astype(BF16), preferred_element_type=F32)
            out.append((mn, l, alpha * acc + pv))
        return tuple(out)

    def body(j, carry):
        off = pl.multiple_of(j * BLK, BLK)
        kj = k_ref[0, pl.ds(off, BLK), :]
        vtj = vt_ref[0, j]
        biases = [bias_ref[h, pl.ds(j, 1), :] for h in range(2)]
        return attend(kj, vtj, biases, carry)

    init = tuple((jnp.full((1, BLK), NEG, F32), jnp.zeros((1, BLK), F32), jnp.zeros((MD, BLK), F32))
                 for _ in range(2))
    carry = lax.fori_loop(0, i, body, init)

    off = pl.multiple_of(i * BLK, BLK)
    kpos = lax.broadcasted_iota(jnp.int32, (BLK, BLK), 0)
    qpos = lax.broadcasted_iota(jnp.int32, (BLK, BLK), 1)
    causal = jnp.where(kpos <= qpos, 0.0, NEG)
    carry = attend(k_ref[0, pl.ds(off, BLK), :], vt_ref[0, i], [causal, causal], carry)

    ot = jnp.concatenate([carry[h][2] / carry[h][1] for h in range(2)], axis=0)
    o_ref[0] = (ot.T * sg_ref[0]).astype(o_ref.dtype)


def _moba(q, k, km, vt, sg):
    B, S, _ = q.shape
    nb = S // BLK
    hp = MW // LANES
    return pl.pallas_call(
        functools.partial(_moba_kernel, nb=nb),
        grid=(B, hp, nb),
        in_specs=[pl.BlockSpec((1, BLK, LANES), lambda b, p, i: (b, i, p)),
                  pl.BlockSpec((1, S, LANES), lambda b, p, i: (b, 0, p)),
                  pl.BlockSpec((1, nb, LANES, BLK), lambda b, p, i: (b, 0, p, 0)),
                  pl.BlockSpec((1, nb, LANES), lambda b, p, i: (b, 0, p)),
                  pl.BlockSpec((1, BLK, LANES), lambda b, p, i: (b, i, p))],
        out_specs=pl.BlockSpec((1, BLK, LANES), lambda b, p, i: (b, i, p)),
        out_shape=jax.ShapeDtypeStruct((B, S, MW), BF16),
        scratch_shapes=[pltpu.VMEM((2, nb, BLK), F32)],
        compiler_params=pltpu.CompilerParams(
            dimension_semantics=("parallel", "parallel", "arbitrary"), vmem_limit_bytes=VMEM_LIMIT),
        name="moba",
    )(q, k, vt, km, sg)


def _dn_kernel(x_ref, sz_ref, gb_ref, gbt_ref, cw_ref, nw_ref, o_ref, xbuf, qkv, state):
    t = pl.program_id(1)
    halo = 8

    @pl.when(t == 0)
    def _():
        xbuf[0:halo, :] = jnp.zeros((halo, 3 * DW), F32)
        state[...] = jnp.zeros_like(state)

    @pl.when(t > 0)
    def _():
        xbuf[0:halo, :] = xbuf[TD:TD + halo, :]

    xbuf[halo:halo + TD, :] = x_ref[0]

    for n in range(3 * DH):
        sl = slice(n * DK, (n + 1) * DK)
        y = jnp.zeros((TD, DK), F32)
        for tap in range(CONV):
            lo = halo - (CONV - 1) + tap
            y = y + cw_ref[tap:tap + 1, sl] * xbuf[lo:lo + TD, sl]
        y = _silu(y)
        if n < 2 * DH:
            y = y * lax.rsqrt(jnp.sum(y * y, axis=-1, keepdims=True) + EPS)
        if n < DH:
            y = y * (DK ** -0.5)
        qkv[:, sl] = y

    r = lax.broadcasted_iota(jnp.int32, (CH, CH), 0)
    c = lax.broadcasted_iota(jnp.int32, (CH, CH), 1)
    tril = r >= c
    tril_strict = r > c
    ones_l = tril.astype(F32)
    ones_u = (r <= c).astype(F32)
    eye = (r == c).astype(F32)
    nw = nw_ref[...]

    def mm(a, b):
        return jnp.dot(a, b, precision=HI, preferred_element_type=F32)

    def chunk(ci, _):
        r0 = pl.multiple_of(ci * CH, CH)
        gcol = gb_ref[0, pl.ds(r0, CH), :]
        grow = gbt_ref[0, ci]
        gc = mm(ones_l, gcol)
        gr = mm(grow, ones_u)
        for h in range(DH):
            q = qkv[pl.ds(r0, CH), h * DK:(h + 1) * DK]
            k = qkv[pl.ds(r0, CH), DW + h * DK:DW + (h + 1) * DK]
            v = qkv[pl.ds(r0, CH), 2 * DW + h * DK:2 * DW + (h + 1) * DK]
            beta = gcol[:, h:h + 1]
            gcum = gc[:, DH + h:DH + h + 1]
            grow_h = gr[DH + h:DH + h + 1, :]
            glast = gcum[CH - 1:CH, :]
            diff = gcum - grow_h
            decay = jnp.where(tril, jnp.exp(jnp.where(tril, diff, 0.0)), 0.0)
            decay_strict = jnp.where(tril_strict, decay, 0.0)
            eg = jnp.exp(gcum)
            kb = k * beta
            a = lax.dot_general(kb, k, NT, precision=HI, preferred_element_type=F32) * decay_strict
            tinv = eye - a
            pw = mm(a, a)
            for lvl in range(5):
                tinv = tinv + mm(tinv, pw)
                if lvl < 4:
                    pw = mm(pw, pw)
            uw = mm(tinv, jnp.concatenate([v * beta, kb * eg], axis=1))
            u, w = uw[:, :DK], uw[:, DK:]
            qk = lax.dot_general(q, k, NT, precision=HI, preferred_element_type=F32) * decay
            st = state[h]
            ws = mm(jnp.concatenate([w, q * eg], axis=0), st)
            v_new = u - ws[:CH]
            o = ws[CH:] + mm(qk, v_new)
            k_dec = k * jnp.exp(glast - gcum)
            state[h] = st * jnp.exp(glast) + lax.dot_general(
                k_dec, v_new, TN, precision=HI, preferred_element_type=F32)
            o = o * lax.rsqrt(jnp.mean(o * o, axis=-1, keepdims=True) + EPS) * nw
            o = o * sz_ref[0, pl.ds(r0, CH), h * DK:(h + 1) * DK]
            o_ref[0, pl.ds(r0, CH), h * DK:(h + 1) * DK] = o.astype(o_ref.dtype)
        return 0

    lax.fori_loop(0, TD // CH, chunk, 0)


def _deltanet(dx, sz, gb, gbt, conv_w, dn_norm_w):
    B, S, _ = dx.shape
    nt = S // TD
    return pl.pallas_call(
        _dn_kernel,
        grid=(B, nt),
        in_specs=[pl.BlockSpec((1, TD, 3 * DW), lambda b, t: (b, t, 0)),
                  pl.BlockSpec((1, TD, DW), lambda b, t: (b, t, 0)),
                  pl.BlockSpec((1, TD, LANES), lambda b, t: (b, t, 0)),
                  pl.BlockSpec((1, TD // CH, 8, CH), lambda b, t: (b, t, 0, 0)),
                  pl.BlockSpec((CONV, 3 * DW), lambda b, t: (0, 0)),
                  pl.BlockSpec((1, DK), lambda b, t: (0, 0))],
        out_specs=pl.BlockSpec((1, TD, DW), lambda b, t: (b, t, 0)),
        out_shape=jax.ShapeDtypeStruct((B, S, DW), BF16),
        scratch_shapes=[pltpu.VMEM((TD + 8, 3 * DW), F32),
                        pltpu.VMEM((TD, 3 * DW), F32),
                        pltpu.VMEM((DH, DK, DK), F32)],
        compiler_params=pltpu.CompilerParams(
            dimension_semantics=("parallel", "arbitrary"), vmem_limit_bytes=VMEM_LIMIT),
        name="deltanet",
    )(dx, sz, gb, gbt, conv_w, dn_norm_w.reshape(1, DK))


def _outproj_kernel(om_ref, od_ref, x_ref, mod_ref, pw_ref, wt_ref, wb_ref, o_ref):
    y = jnp.dot(om_ref[0], wt_ref[...], preferred_element_type=F32)
    y = y + jnp.dot(od_ref[0], wb_ref[...], preferred_element_type=F32)
    yn = y * lax.rsqrt(jnp.mean(y * y, axis=-1, keepdims=True) + EPS) * pw_ref[...]
    o_ref[0] = x_ref[0] + mod_ref[0, 2:3, :] * yn


def _outproj(om, od, x, mod3, post_w, w_out):
    B, S, _ = x.shape
    wb = w_out.astype(BF16)
    row = lambda w: pl.BlockSpec((1, TO, w), lambda b, t: (b, t, 0))
    return pl.pallas_call(
        _outproj_kernel,
        grid=(B, S // TO),
        in_specs=[row(MW), row(DW), row(D),
                  pl.BlockSpec((1, 3, D), lambda b, t: (b, 0, 0)),
                  pl.BlockSpec((1, D), lambda b, t: (0, 0)),
                  pl.BlockSpec((MW, D), lambda b, t: (0, 0)),
                  pl.BlockSpec((DW, D), lambda b, t: (0, 0))],
        out_specs=row(D),
        out_shape=jax.ShapeDtypeStruct((B, S, D), F32),
        compiler_params=pltpu.CompilerParams(
            dimension_semantics=("parallel", "parallel"), vmem_limit_bytes=VMEM_LIMIT),
        name="outproj",
    )(om, od, x, mod3, post_w.reshape(1, D), wb[:MW], wb[MW:])


def kernel(x, c, ada_w, ada_b, pre_norm_w, post_norm_w, w_in, conv_w, a_log, dt_bias, dn_norm_w, w_out):
    B, S, _ = x.shape
    assert S % TD == 0 and S % TO == 0 and S % BLK == 0 and TM == BLK
    depth = ada_w.shape[0]
    for l in range(depth):
        mod3 = _mod(c, ada_w[l], ada_b[l]).reshape(B, 3, D)
        q, k, km, vt, sg, dx, sz, gb, gbt = _inproj(x, mod3, pre_norm_w[l], w_in[l], a_log[l], dt_bias[l])
        om = _moba(q, k, km.reshape(B, S // BLK, MW), vt, sg)
        od = _deltanet(dx, sz, gb, gbt, conv_w[l], dn_norm_w[l])
        x = _outproj(om, od, x, mod3, post_norm_w[l], w_out[l])
    return x
```

```python
import functools
import math

import jax
import jax.numpy as jnp
from jax import lax
from jax.experimental import pallas as pl
from jax.experimental.pallas import tpu as pltpu

F32 = jnp.float32
BF16 = jnp.bfloat16
HI = lax.Precision.HIGHEST

D = 1024
MH, MD = 8, 64
MW = MH * MD
BLK = 256
TOPK = 3
ROPE_D = 16
ROPE_THETA = 500000.0
DH, DK = 4, 128
DW = DH * DK
CONV = 4
CH = 64
EPS = 1e-6
NEG = -1e30
LOWEST = -3.0e38

LANES = 128
VMEM_LIMIT = 56 * 1024 * 1024

TM = 256
TD = 512
CU = 2
TO = 512

NT = (((1,), (1,)), ((), ()))
TN = (((0,), (0,)), ((), ()))


def _silu(v):
    return v * jax.nn.sigmoid(v)


def _mod_kernel(c_ref, w_ref, b_ref, o_ref):
    c = c_ref[...]
    o_ref[...] = jnp.dot(_silu(c), w_ref[...], precision=HI, preferred_element_type=F32) + b_ref[...]


def _mod(c, ada_w, ada_b):
    B = c.shape[0]
    tn = 512
    return pl.pallas_call(
        _mod_kernel,
        grid=(3 * D // tn,),
        in_specs=[pl.BlockSpec((B, D), lambda j: (0, 0)),
                  pl.BlockSpec((D, tn), lambda j: (0, j)),
                  pl.BlockSpec((1, tn), lambda j: (0, j))],
        out_specs=pl.BlockSpec((B, tn), lambda j: (0, j)),
        out_shape=jax.ShapeDtypeStruct((B, 3 * D), F32),
        name="mod",
    )(c, ada_w, ada_b.reshape(1, 3 * D))


def _inproj_kernel(x_ref, mod_ref, pw_ref, wqk_ref, wvt_ref, wg_ref, wd_ref, wz_ref, ws_ref,
                   rc_ref, rs1_ref, rs2_ref, alog_ref, dtb_ref,
                   q_ref, k_ref, km_ref, vt_ref, sg_ref, dx_ref, sz_ref, gb_ref, gbt_ref):
    x = x_ref[0]
    shift = mod_ref[0, 0:1, :]
    scale = mod_ref[0, 1:2, :]
    y = x * lax.rsqrt(jnp.mean(x * x, axis=-1, keepdims=True) + EPS) * pw_ref[...]
    h = (y * (1.0 + scale) + shift).astype(BF16)

    rc, rs1, rs2 = rc_ref[...], rs1_ref[...], rs2_ref[...]

    def rope(t):
        tm1 = pltpu.roll(t, MW - ROPE_D // 2, axis=1)
        tp1 = pltpu.roll(t, ROPE_D // 2, axis=1)
        parts = []
        for n in range(MW // LANES):
            sl = slice(n * LANES, (n + 1) * LANES)
            parts.append(t[:, sl] * rc + tm1[:, sl] * rs1 + tp1[:, sl] * rs2)
        return jnp.concatenate(parts, axis=1)

    qk = jnp.dot(h, wqk_ref[...], preferred_element_type=F32)
    q_ref[0] = rope(qk[:, :MW])
    k = rope(qk[:, MW:])
    k_ref[0] = k.astype(BF16)
    km_ref[0, 0] = jnp.mean(k, axis=0, keepdims=True)

    vt = lax.dot_general(wvt_ref[...], h, NT, preferred_element_type=F32)
    vt_ref[0, 0] = vt.astype(BF16)

    sg_ref[0] = _silu(jnp.dot(h, wg_ref[...], preferred_element_type=F32))
    dx_ref[0] = jnp.dot(h, wd_ref[...], preferred_element_type=F32)
    sz_ref[0] = _silu(jnp.dot(h, wz_ref[...], preferred_element_type=F32))

    s = jnp.dot(h, ws_ref[...], preferred_element_type=F32)
    lane = lax.broadcasted_iota(jnp.int32, s.shape, 1)
    beta = jax.nn.sigmoid(s)
    z = s + dtb_ref[...]
    softplus = jnp.maximum(z, 0.0) + jnp.log1p(jnp.exp(-jnp.abs(z)))
    g = -jnp.exp(alog_ref[...]) * softplus
    gb = jnp.where(lane < DH, beta, jnp.where(lane < 2 * DH, g, 0.0))
    gb_ref[0] = gb
    gbt = gb.T[0:8, :]
    for cidx in range(TM // CH):
        gbt_ref[0, cidx] = gbt[:, cidx * CH:(cidx + 1) * CH]


def _inproj(x, mod3, pre_w, w_in, a_log, dt_bias):
    B, S, _ = x.shape
    nb = S // BLK
    nt = S // TM
    wb = w_in.astype(BF16)
    wqk = wb[:, 0:2 * MW]
    wvt = wb[:, 2 * MW:3 * MW].T
    wg = wb[:, 3 * MW:4 * MW]
    wd = wb[:, 4 * MW:4 * MW + 3 * DW]
    wz = wb[:, 4 * MW + 3 * DW:4 * MW + 4 * DW]
    ws = jnp.pad(wb[:, 4 * MW + 4 * DW:], ((0, 0), (0, LANES - 2 * DH)))
    alog = jnp.pad(a_log.astype(F32), (DH, LANES - 2 * DH)).reshape(1, LANES)
    dtb = jnp.pad(dt_bias.astype(F32), (DH, LANES - 2 * DH)).reshape(1, LANES)

    half = ROPE_D // 2
    inv_freq = jnp.power(ROPE_THETA, -jnp.arange(0, ROPE_D, 2, dtype=F32) / ROPE_D)
    ang = jnp.arange(S, dtype=jnp.int32).astype(F32)[:, None] * inv_freq[None, :]
    cos, sin = jnp.cos(ang), jnp.sin(ang)
    one = jnp.ones((S, MD - ROPE_D), F32)
    zero = jnp.zeros((S, MD - ROPE_D), F32)
    zh = jnp.zeros((S, half), F32)
    rc = jnp.tile(jnp.concatenate([cos, cos, one], axis=1), (1, 2))
    rs1 = jnp.tile(jnp.concatenate([-sin, zh, zero], axis=1), (1, 2))
    rs2 = jnp.tile(jnp.concatenate([zh, sin, zero], axis=1), (1, 2))

    const = lambda shape: pl.BlockSpec(shape, lambda b, t: (0,) * len(shape))
    row = lambda w: pl.BlockSpec((1, TM, w), lambda b, t: (b, t, 0))
    tab = pl.BlockSpec((TM, LANES), lambda b, t: (t, 0))
    out_shapes = [
        jax.ShapeDtypeStruct((B, S, MW), F32),
        jax.ShapeDtypeStruct((B, S, MW), BF16),
        jax.ShapeDtypeStruct((B, nb, 1, MW), F32),
        jax.ShapeDtypeStruct((B, nb, MW, BLK), BF16),
        jax.ShapeDtypeStruct((B, S, MW), F32),
        jax.ShapeDtypeStruct((B, S, 3 * DW), F32),
        jax.ShapeDtypeStruct((B, S, DW), F32),
        jax.ShapeDtypeStruct((B, S, LANES), F32),
        jax.ShapeDtypeStruct((B, S // CH, 8, CH), F32),
    ]
    out_specs = [
        row(MW), row(MW),
        pl.BlockSpec((1, 1, 1, MW), lambda b, t: (b, t, 0, 0)),
        pl.BlockSpec((1, 1, MW, BLK), lambda b, t: (b, t, 0, 0)),
        row(MW), row(3 * DW), row(DW), row(LANES),
        pl.BlockSpec((1, TM // CH, 8, CH), lambda b, t: (b, t, 0, 0)),
    ]
    return pl.pallas_call(
        _inproj_kernel,
        grid=(B, nt),
        in_specs=[row(D), pl.BlockSpec((1, 3, D), lambda b, t: (b, 0, 0)), const((1, D)),
                  const((D, 2 * MW)), const((MW, D)), const((D, MW)), const((D, 3 * DW)),
                  const((D, DW)), const((D, LANES)), tab, tab, tab,
                  const((1, LANES)), const((1, LANES))],
        out_specs=out_specs,
        out_shape=out_shapes,
        compiler_params=pltpu.CompilerParams(
            dimension_semantics=("parallel", "parallel"), vmem_limit_bytes=VMEM_LIMIT),
        name="inproj",
    )(x, mod3, pre_w.reshape(1, D), wqk, wvt, wg, wd, wz, ws, rc, rs1, rs2, alog, dtb)


def _moba_kernel(q_ref, k_ref, vt_ref, km_ref, sg_ref, o_ref, bias_ref, *, nb):
    i = pl.program_id(2)
    q = q_ref[0]
    km = km_ref[0]
    lane = lax.broadcasted_iota(jnp.int32, (1, LANES), 1)
    head_lanes = [lane < MD, lane >= MD]
    blk = lax.broadcasted_iota(jnp.int32, (nb, BLK), 0)
    past = blk < i
    scale = MD ** -0.5

    qs = []
    for h in range(2):
        kmh = jnp.where(head_lanes[h], km, 0.0)
        gate = lax.dot_general(kmh, q, NT, precision=HI, preferred_element_type=F32)
        g = jnp.where(past, gate, NEG)
        sel = jnp.zeros((nb, BLK), jnp.bool_)
        for _ in range(min(TOPK, nb)):
            m = jnp.max(g, axis=0, keepdims=True)
            idx = jnp.min(jnp.where(g == m, blk, nb), axis=0, keepdims=True)
            pick = blk == idx
            sel = jnp.logical_or(sel, pick)
            g = jnp.where(pick, LOWEST, g)
        bias_ref[h] = jnp.where(jnp.logical_and(sel, past), 0.0, NEG)
        qs.append((jnp.where(head_lanes[h], q, 0.0) * scale).astype(BF16))

    def attend(kj, vtj, biases, carry):
        out = []
        for h in range(2):
            m, l, acc = carry[h]
            s = lax.dot_general(kj, qs[h], NT, preferred_element_type=F32) + biases[h]
            mn = jnp.maximum(m, jnp.max(s, axis=0, keepdims=True))
            alpha = jnp.exp(m - mn)
            p = jnp.exp(s - mn)
            l = alpha * l + jnp.sum(p, axis=0, keepdims=True)
            pv = jnp.dot(vtj[h * MD:(h + 1) * MD, :], p.astype(BF16), preferred_element_type=F32)
            out.append((mn, l, alpha * acc + pv))
        return tuple(out)

    def body(j, carry):
        off = pl.multiple_of(j * BLK, BLK)
        kj = k_ref[0, pl.ds(off, BLK), :]
        vtj = vt_ref[0, j]
        biases = [bias_ref[h, pl.ds(j, 1), :] for h in range(2)]
        return attend(kj, vtj, biases, carry)

    init = tuple((jnp.full((1, BLK), NEG, F32), jnp.zeros((1, BLK), F32), jnp.zeros((MD, BLK), F32))
                 for _ in range(2))
    carry = lax.fori_loop(0, i, body, init)

    off = pl.multiple_of(i * BLK, BLK)
    kpos = lax.broadcasted_iota(jnp.int32, (BLK, BLK), 0)
    qpos = lax.broadcasted_iota(jnp.int32, (BLK, BLK), 1)
    causal = jnp.where(kpos <= qpos, 0.0, NEG)
    carry = attend(k_ref[0, pl.ds(off, BLK), :], vt_ref[0, i], [causal, causal], carry)

    ot = jnp.concatenate([carry[h][2] / carry[h][1] for h in range(2)], axis=0)
    o_ref[0] = (ot.T * sg_ref[0]).astype(o_ref.dtype)


def _moba(q, k, km, vt, sg):
    B, S, _ = q.shape
    nb = S // BLK
    hp = MW // LANES
    return pl.pallas_call(
        functools.partial(_moba_kernel, nb=nb),
        grid=(B, hp, nb),
        in_specs=[pl.BlockSpec((1, BLK, LANES), lambda b, p, i: (b, i, p)),
                  pl.BlockSpec((1, S, LANES), lambda b, p, i: (b, 0, p)),
                  pl.BlockSpec((1, nb, LANES, BLK), lambda b, p, i: (b, 0, p, 0)),
                  pl.BlockSpec((1, nb, LANES), lambda b, p, i: (b, 0, p)),
                  pl.BlockSpec((1, BLK, LANES), lambda b, p, i: (b, i, p))],
        out_specs=pl.BlockSpec((1, BLK, LANES), lambda b, p, i: (b, i, p)),
        out_shape=jax.ShapeDtypeStruct((B, S, MW), BF16),
        scratch_shapes=[pltpu.VMEM((2, nb, BLK), F32)],
        compiler_params=pltpu.CompilerParams(
            dimension_semantics=("parallel", "parallel", "arbitrary"), vmem_limit_bytes=VMEM_LIMIT),
        name="moba",
    )(q, k, vt, km, sg)


def _split_bf16(p):
    hi = p.astype(BF16).astype(F32)
    lo = (p - hi).astype(BF16).astype(F32)
    return hi, lo


def _dn_kernel(x_ref, sz_ref, gb_ref, gbt_ref, cw_ref, nw_ref, o_ref, xbuf, qkv, state, mqs, cs, os_, gls):
    t = pl.program_id(1)
    halo = 8

    @pl.when(t == 0)
    def _():
        xbuf[0:halo, :] = jnp.zeros((halo, 3 * DW), F32)
        state[...] = jnp.zeros_like(state)

    @pl.when(t > 0)
    def _():
        xbuf[0:halo, :] = xbuf[TD:TD + halo, :]

    xbuf[halo:halo + TD, :] = x_ref[0]

    for n in range(3 * DH):
        sl = slice(n * DK, (n + 1) * DK)
        y = jnp.zeros((TD, DK), F32)
        for tap in range(CONV):
            lo = halo - (CONV - 1) + tap
            y = y + cw_ref[tap:tap + 1, sl] * xbuf[lo:lo + TD, sl]
        y = _silu(y)
        if n < 2 * DH:
            y = y * lax.rsqrt(jnp.sum(y * y, axis=-1, keepdims=True) + EPS)
        if n < DH:
            y = y * (DK ** -0.5)
        qkv[:, sl] = y

    r = lax.broadcasted_iota(jnp.int32, (CH, CH), 0)
    c = lax.broadcasted_iota(jnp.int32, (CH, CH), 1)
    tril = r >= c
    tril_strict = r > c
    ones_l = tril.astype(F32)
    ones_u = (r <= c).astype(F32)
    nw = nw_ref[...]

    def local(it, _):
        units = [(dc, h) for dc in range(CU) for h in range(DH)]
        ci = [it * CU + dc for dc in range(CU)]
        r0 = [pl.multiple_of(c_ * CH, CH) for c_ in ci]
        gcol = [gb_ref[0, pl.ds(r0[dc], CH), :] for dc in range(CU)]
        grow = [gbt_ref[0, ci[dc]] for dc in range(CU)]
        gc = [jnp.dot(ones_l, g, precision=HI, preferred_element_type=F32) for g in gcol]
        gr = [jnp.dot(g, ones_u, precision=HI, preferred_element_type=F32) for g in grow]
        q = [qkv[pl.ds(r0[dc], CH), h * DK:(h + 1) * DK] for dc, h in units]
        k = [qkv[pl.ds(r0[dc], CH), DW + h * DK:DW + (h + 1) * DK] for dc, h in units]
        v = [qkv[pl.ds(r0[dc], CH), 2 * DW + h * DK:2 * DW + (h + 1) * DK] for dc, h in units]
        beta = [gcol[dc][:, h:h + 1] for dc, h in units]
        gcum = [gc[dc][:, DH + h:DH + h + 1] for dc, h in units]
        glast = [g[CH - 1:CH, :] for g in gcum]
        us = range(len(units))
        decay = [jnp.where(tril, jnp.exp(jnp.where(
            tril, gcum[u] - gr[units[u][0]][DH + units[u][1]:DH + units[u][1] + 1, :], 0.0)), 0.0) for u in us]
        eg = [jnp.exp(g) for g in gcum]
        kb = [k[u] * beta[u] for u in us]
        aq = [lax.dot_general(jnp.concatenate([kb[u], q[u]], axis=0).astype(BF16), k[u].astype(BF16), NT,
                              preferred_element_type=F32) for u in us]
        p = [jnp.where(tril_strict, aq[u][:CH] * decay[u], 0.0) for u in us]
        qk = [aq[u][CH:] * decay[u] for u in us]
        xs = [jnp.concatenate([v[u] * beta[u], kb[u] * eg[u]], axis=1) for u in us]
        for lvl in range(6):
            for u in us:
                ph, plo = _split_bf16(p[u])
                xh, xlo = _split_bf16(xs[u])
                lhs = jnp.concatenate([ph, plo, ph], axis=1).astype(BF16)
                px = jnp.dot(lhs, jnp.concatenate([xh, xh, xlo], axis=0).astype(BF16),
                             preferred_element_type=F32)
                xs[u] = xs[u] - px if lvl == 0 else xs[u] + px
                if lvl < 5:
                    p[u] = jnp.dot(lhs, jnp.concatenate([ph, ph, plo], axis=0).astype(BF16),
                                   preferred_element_type=F32)
        xb = [x.astype(BF16) for x in xs]
        kdt = [(k[u] * jnp.exp(glast[u] - gcum[u])).T.astype(BF16) for u in us]
        cm = [jnp.dot(kdt[u], xb[u], preferred_element_type=F32) for u in us]
        qo = [jnp.dot(qk[u].astype(BF16), xb[u], preferred_element_type=F32) for u in us]
        for dc in range(CU):
            gls[ci[dc]] = jnp.exp(gc[dc][CH - 1:CH, :])
        for u, (dc, h) in enumerate(units):
            mqs[h, ci[dc], 0:DK, :] = cm[u][:, DK:].astype(BF16)
            mqs[h, ci[dc], DK:DK + CH, :] = (q[u] * eg[u] - qo[u][:, DK:]).astype(BF16)
            cs[h, ci[dc]] = cm[u][:, :DK]
            os_[pl.ds(r0[dc], CH), h * DK:(h + 1) * DK] = qo[u][:, :DK]
        return 0

    lax.fori_loop(0, TD // (CH * CU), local, 0)

    def scan(ci, _):
        r0 = pl.multiple_of(ci * CH, CH)
        gl_all = gls[ci]
        for h in range(DH):
            st = state[h]
            rr = jnp.dot(mqs[h, ci], st.astype(BF16), preferred_element_type=F32)
            state[h] = st * gl_all[:, DH + h:DH + h + 1] - rr[:DK] + cs[h, ci]
            o = rr[DK:] + os_[pl.ds(r0, CH), h * DK:(h + 1) * DK]
            o = o * lax.rsqrt(jnp.mean(o * o, axis=-1, keepdims=True) + EPS) * nw
            o = o * sz_ref[0, pl.ds(r0, CH), h * DK:(h + 1) * DK]
            o_ref[0, pl.ds(r0, CH), h * DK:(h + 1) * DK] = o.astype(o_ref.dtype)
        return 0

    lax.fori_loop(0, TD // CH, scan, 0)


def _deltanet(dx, sz, gb, gbt, conv_w, dn_norm_w):
    B, S, _ = dx.shape
    nt = S // TD
    return pl.pallas_call(
        _dn_kernel,
        grid=(B, nt),
        in_specs=[pl.BlockSpec((1, TD, 3 * DW), lambda b, t: (b, t, 0)),
                  pl.BlockSpec((1, TD, DW), lambda b, t: (b, t, 0)),
                  pl.BlockSpec((1, TD, LANES), lambda b, t: (b, t, 0)),
                  pl.BlockSpec((1, TD // CH, 8, CH), lambda b, t: (b, t, 0, 0)),
                  pl.BlockSpec((CONV, 3 * DW), lambda b, t: (0, 0)),
                  pl.BlockSpec((1, DK), lambda b, t: (0, 0))],
        out_specs=pl.BlockSpec((1, TD, DW), lambda b, t: (b, t, 0)),
        out_shape=jax.ShapeDtypeStruct((B, S, DW), BF16),
        scratch_shapes=[pltpu.VMEM((TD + 8, 3 * DW), F32),
                        pltpu.VMEM((TD, 3 * DW), F32),
                        pltpu.VMEM((DH, DK, DK), F32),
                        pltpu.VMEM((DH, TD // CH, DK + CH, DK), BF16),
                        pltpu.VMEM((DH, TD // CH, DK, DK), F32),
                        pltpu.VMEM((TD, DW), F32),
                        pltpu.VMEM((TD // CH, 1, LANES), F32)],
        compiler_params=pltpu.CompilerParams(
            dimension_semantics=("parallel", "arbitrary"), vmem_limit_bytes=VMEM_LIMIT),
        name="deltanet",
    )(dx, sz, gb, gbt, conv_w, dn_norm_w.reshape(1, DK))


def _outproj_kernel(om_ref, od_ref, x_ref, mod_ref, pw_ref, wt_ref, wb_ref, o_ref):
    y = jnp.dot(om_ref[0], wt_ref[...], preferred_element_type=F32)
    y = y + jnp.dot(od_ref[0], wb_ref[...], preferred_element_type=F32)
    yn = y * lax.rsqrt(jnp.mean(y * y, axis=-1, keepdims=True) + EPS) * pw_ref[...]
    o_ref[0] = x_ref[0] + mod_ref[0, 2:3, :] * yn


def _outproj(om, od, x, mod3, post_w, w_out):
    B, S, _ = x.shape
    wb = w_out.astype(BF16)
    row = lambda w: pl.BlockSpec((1, TO, w), lambda b, t: (b, t, 0))
    return pl.pallas_call(
        _outproj_kernel,
        grid=(B, S // TO),
        in_specs=[row(MW), row(DW), row(D),
                  pl.BlockSpec((1, 3, D), lambda b, t: (b, 0, 0)),
                  pl.BlockSpec((1, D), lambda b, t: (0, 0)),
                  pl.BlockSpec((MW, D), lambda b, t: (0, 0)),
                  pl.BlockSpec((DW, D), lambda b, t: (0, 0))],
        out_specs=row(D),
        out_shape=jax.ShapeDtypeStruct((B, S, D), F32),
        compiler_params=pltpu.CompilerParams(
            dimension_semantics=("parallel", "parallel"), vmem_limit_bytes=VMEM_LIMIT),
        name="outproj",
    )(om, od, x, mod3, post_w.reshape(1, D), wb[:MW], wb[MW:])


def kernel(x, c, ada_w, ada_b, pre_norm_w, post_norm_w, w_in, conv_w, a_log, dt_bias, dn_norm_w, w_out):
    B, S, _ = x.shape
    assert S % TD == 0 and S % TO == 0 and S % BLK == 0 and TM == BLK
    depth = ada_w.shape[0]
    for l in range(depth):
        mod3 = _mod(c, ada_w[l], ada_b[l]).reshape(B, 3, D)
        q, k, km, vt, sg, dx, sz, gb, gbt = _inproj(x, mod3, pre_norm_w[l], w_in[l], a_log[l], dt_bias[l])
        om = _moba(q, k, km.reshape(B, S // BLK, MW), vt, sg)
        od = _deltanet(dx, sz, gb, gbt, conv_w[l], dn_norm_w[l])
        x = _outproj(om, od, x, mod3, post_norm_w[l], w_out[l])
    return x
```

```python
import functools
import math

import jax
import jax.numpy as jnp
from jax import lax
from jax.experimental import pallas as pl
from jax.experimental.pallas import tpu as pltpu

F32 = jnp.float32
BF16 = jnp.bfloat16
HI = lax.Precision.HIGHEST

D = 1024
MH, MD = 8, 64
MW = MH * MD
BLK = 256
TOPK = 3
ROPE_D = 16
ROPE_THETA = 500000.0
DH, DK = 4, 128
DW = DH * DK
CONV = 4
CH = 64
EPS = 1e-6
NEG = -1e30
LOWEST = -3.0e38
LOG2E = math.log2(math.e)
PVG = 2

LANES = 128
VMEM_LIMIT = 56 * 1024 * 1024

TM = 256
TD = 512
CU = 2
TO = 512

NT = (((1,), (1,)), ((), ()))
TN = (((0,), (0,)), ((), ()))


def _silu(v):
    return v * jax.nn.sigmoid(v)


def _mod_kernel(c_ref, w_ref, b_ref, o_ref):
    c = c_ref[...]
    o_ref[...] = jnp.dot(_silu(c), w_ref[...], precision=HI, preferred_element_type=F32) + b_ref[...]


def _mod(c, ada_w, ada_b):
    B = c.shape[0]
    tn = 512
    return pl.pallas_call(
        _mod_kernel,
        grid=(3 * D // tn,),
        in_specs=[pl.BlockSpec((B, D), lambda j: (0, 0)),
                  pl.BlockSpec((D, tn), lambda j: (0, j)),
                  pl.BlockSpec((1, tn), lambda j: (0, j))],
        out_specs=pl.BlockSpec((B, tn), lambda j: (0, j)),
        out_shape=jax.ShapeDtypeStruct((B, 3 * D), F32),
        name="mod",
    )(c, ada_w, ada_b.reshape(1, 3 * D))


def _inproj_kernel(x_ref, mod_ref, pw_ref, wqk_ref, wvt_ref, wg_ref, wd_ref, wz_ref, ws_ref,
                   rc_ref, rs1_ref, rs2_ref, alog_ref, dtb_ref,
                   q_ref, k_ref, km_ref, vt_ref, sg_ref, dx_ref, sz_ref, gb_ref, gbt_ref):
    x = x_ref[0]
    shift = mod_ref[0, 0:1, :]
    scale = mod_ref[0, 1:2, :]
    y = x * lax.rsqrt(jnp.mean(x * x, axis=-1, keepdims=True) + EPS) * pw_ref[...]
    h = (y * (1.0 + scale) + shift).astype(BF16)

    rc, rs1, rs2 = rc_ref[...], rs1_ref[...], rs2_ref[...]

    def rope(t):
        tm1 = pltpu.roll(t, MW - ROPE_D // 2, axis=1)
        tp1 = pltpu.roll(t, ROPE_D // 2, axis=1)
        parts = []
        for n in range(MW // LANES):
            sl = slice(n * LANES, (n + 1) * LANES)
            parts.append(t[:, sl] * rc + tm1[:, sl] * rs1 + tp1[:, sl] * rs2)
        return jnp.concatenate(parts, axis=1)

    qk = jnp.dot(h, wqk_ref[...], preferred_element_type=F32)
    q_ref[0] = rope(qk[:, :MW])
    k = rope(qk[:, MW:])
    k_ref[0] = k.astype(BF16)
    km_ref[0, 0] = jnp.mean(k, axis=0, keepdims=True)

    vt = lax.dot_general(wvt_ref[...], h, NT, preferred_element_type=F32)
    vt_ref[0, 0] = vt.astype(BF16)

    sg_ref[0] = _silu(jnp.dot(h, wg_ref[...], preferred_element_type=F32))
    dx_ref[0] = jnp.dot(h, wd_ref[...], preferred_element_type=F32)
    sz_ref[0] = _silu(jnp.dot(h, wz_ref[...], preferred_element_type=F32))

    s = jnp.dot(h, ws_ref[...], preferred_element_type=F32)
    lane = lax.broadcasted_iota(jnp.int32, s.shape, 1)
    beta = jax.nn.sigmoid(s)
    z = s + dtb_ref[...]
    softplus = jnp.maximum(z, 0.0) + jnp.log1p(jnp.exp(-jnp.abs(z)))
    g = -jnp.exp(alog_ref[...]) * softplus
    gb = jnp.where(lane < DH, beta, jnp.where(lane < 2 * DH, g, 0.0))
    gb_ref[0] = gb
    gbt = gb.T[0:8, :]
    for cidx in range(TM // CH):
        gbt_ref[0, cidx] = gbt[:, cidx * CH:(cidx + 1) * CH]


def _inproj(x, mod3, pre_w, w_in, a_log, dt_bias):
    B, S, _ = x.shape
    nb = S // BLK
    nt = S // TM
    wb = w_in.astype(BF16)
    wqk = wb[:, 0:2 * MW]
    wvt = wb[:, 2 * MW:3 * MW].T
    wg = wb[:, 3 * MW:4 * MW]
    wd = wb[:, 4 * MW:4 * MW + 3 * DW]
    wz = wb[:, 4 * MW + 3 * DW:4 * MW + 4 * DW]
    ws = jnp.pad(wb[:, 4 * MW + 4 * DW:], ((0, 0), (0, LANES - 2 * DH)))
    alog = jnp.pad(a_log.astype(F32), (DH, LANES - 2 * DH)).reshape(1, LANES)
    dtb = jnp.pad(dt_bias.astype(F32), (DH, LANES - 2 * DH)).reshape(1, LANES)

    half = ROPE_D // 2
    inv_freq = jnp.power(ROPE_THETA, -jnp.arange(0, ROPE_D, 2, dtype=F32) / ROPE_D)
    ang = jnp.arange(S, dtype=jnp.int32).astype(F32)[:, None] * inv_freq[None, :]
    cos, sin = jnp.cos(ang), jnp.sin(ang)
    one = jnp.ones((S, MD - ROPE_D), F32)
    zero = jnp.zeros((S, MD - ROPE_D), F32)
    zh = jnp.zeros((S, half), F32)
    rc = jnp.tile(jnp.concatenate([cos, cos, one], axis=1), (1, 2))
    rs1 = jnp.tile(jnp.concatenate([-sin, zh, zero], axis=1), (1, 2))
    rs2 = jnp.tile(jnp.concatenate([zh, sin, zero], axis=1), (1, 2))

    const = lambda shape: pl.BlockSpec(shape, lambda b, t: (0,) * len(shape))
    row = lambda w: pl.BlockSpec((1, TM, w), lambda b, t: (b, t, 0))
    tab = pl.BlockSpec((TM, LANES), lambda b, t: (t, 0))
    out_shapes = [
        jax.ShapeDtypeStruct((B, S, MW), F32),
        jax.ShapeDtypeStruct((B, S, MW), BF16),
        jax.ShapeDtypeStruct((B, nb, 1, MW), F32),
        jax.ShapeDtypeStruct((B, nb // PVG, MW, PVG * BLK), BF16),
        jax.ShapeDtypeStruct((B, S, MW), F32),
        jax.ShapeDtypeStruct((B, S, 3 * DW), F32),
        jax.ShapeDtypeStruct((B, S, DW), F32),
        jax.ShapeDtypeStruct((B, S, LANES), F32),
        jax.ShapeDtypeStruct((B, S // CH, 8, CH), F32),
    ]
    out_specs = [
        row(MW), row(MW),
        pl.BlockSpec((1, 1, 1, MW), lambda b, t: (b, t, 0, 0)),
        pl.BlockSpec((1, 1, MW, BLK), lambda b, t: (b, t // PVG, 0, t % PVG)),
        row(MW), row(3 * DW), row(DW), row(LANES),
        pl.BlockSpec((1, TM // CH, 8, CH), lambda b, t: (b, t, 0, 0)),
    ]
    return pl.pallas_call(
        _inproj_kernel,
        grid=(B, nt),
        in_specs=[row(D), pl.BlockSpec((1, 3, D), lambda b, t: (b, 0, 0)), const((1, D)),
                  const((D, 2 * MW)), const((MW, D)), const((D, MW)), const((D, 3 * DW)),
                  const((D, DW)), const((D, LANES)), tab, tab, tab,
                  const((1, LANES)), const((1, LANES))],
        out_specs=out_specs,
        out_shape=out_shapes,
        compiler_params=pltpu.CompilerParams(
            dimension_semantics=("parallel", "parallel"), vmem_limit_bytes=VMEM_LIMIT),
        name="inproj",
    )(x, mod3, pre_w.reshape(1, D), wqk, wvt, wg, wd, wz, ws, rc, rs1, rs2, alog, dtb)


def _moba_kernel(q_ref, k_ref, vt_ref, km_ref, sg_ref, o_ref, bias_ref, pm_ref, s_ref, p_ref, *, nb):
    i = pl.program_id(2)
    q = q_ref[0]
    km = km_ref[0]
    lane = lax.broadcasted_iota(jnp.int32, (1, LANES), 1)
    head_lanes = [lane < MD, lane >= MD]
    blk = lax.broadcasted_iota(jnp.int32, (nb, BLK), 0)
    past = blk < i
    scale = MD ** -0.5

    qs = []
    for h in range(2):
        kmh = jnp.where(head_lanes[h], km, 0.0)
        gate = lax.dot_general(kmh, q, NT, precision=HI, preferred_element_type=F32)
        g = jnp.where(past, gate, NEG)
        sel = jnp.zeros((nb, BLK), jnp.bool_)
        for _ in range(min(TOPK, nb)):
            m = jnp.max(g, axis=0, keepdims=True)
            idx = jnp.min(jnp.where(g == m, blk, nb), axis=0, keepdims=True)
            pick = blk == idx
            sel = jnp.logical_or(sel, pick)
            g = jnp.where(pick, LOWEST, g)
        bias_ref[h] = jnp.where(jnp.logical_or(jnp.logical_and(sel, past), blk == i), 0.0, NEG)
        qs.append((jnp.where(head_lanes[h], q, 0.0) * (scale * LOG2E)).astype(BF16))

    def fold8(a):
        return a.reshape(BLK // 8, 8, BLK)

    @pl.when(i == 0)
    def _():
        kpos = lax.broadcasted_iota(jnp.int32, (BLK, BLK), 0)
        qpos = lax.broadcasted_iota(jnp.int32, (BLK, BLK), 1)
        pm_ref[0] = jnp.zeros((BLK, BLK), F32)
        pm_ref[1] = jnp.where(kpos <= qpos, 0.0, NEG)
        pm_ref[2] = jnp.full((BLK, BLK), NEG, F32)

    ngroups = i // PVG + 1

    def scores(t, mx):
        mx = list(mx)
        for dj in range(PVG):
            j = t * PVG + dj
            kj = k_ref[0, pl.ds(pl.multiple_of(j * BLK, BLK), BLK), :]
            pm = pm_ref[jnp.clip(j - i, -1, 1) + 1]
            for h in range(2):
                s = lax.dot_general(kj, qs[h], NT, preferred_element_type=F32) + pm
                s_ref[h, j] = s
                mx[h] = jnp.maximum(mx[h], jnp.max(fold8(s), axis=0) + bias_ref[h, pl.ds(j, 1), :])
        return tuple(mx)

    mx = lax.fori_loop(0, ngroups, scores, tuple(jnp.full((8, BLK), NEG, F32) for _ in range(2)))
    m = [jnp.max(mx[h], axis=0, keepdims=True) for h in range(2)]

    def probs(t, ls):
        ls = list(ls)
        for dj in range(PVG):
            j = t * PVG + dj
            for h in range(2):
                p = jnp.exp2(s_ref[h, j] - (m[h] - bias_ref[h, pl.ds(j, 1), :]))
                p_ref[h, pl.ds(pl.multiple_of(j * BLK, BLK), BLK), :] = p.astype(BF16)
                ls[h] = ls[h] + jnp.sum(fold8(p), axis=0)
        return tuple(ls)

    def pv(t, accs):
        r0 = pl.multiple_of(t * (PVG * BLK), PVG * BLK)
        return tuple(accs[h] + jnp.dot(vt_ref[0, t, h * MD:(h + 1) * MD, :], p_ref[h, pl.ds(r0, PVG * BLK), :],
                                       preferred_element_type=F32) for h in range(2))

    def step(t, carry):
        ls, accs = carry
        accs = pv(t - 1, accs)
        return probs(t, ls), accs

    ls = probs(0, tuple(jnp.zeros((8, BLK), F32) for _ in range(2)))
    ls, accs = lax.fori_loop(1, ngroups, step, (ls, tuple(jnp.zeros((MD, BLK), F32) for _ in range(2))))
    accs = pv(ngroups - 1, accs)

    ot = jnp.concatenate([accs[h] / jnp.sum(ls[h], axis=0, keepdims=True) for h in range(2)], axis=0)
    o_ref[0] = (ot.T * sg_ref[0]).astype(o_ref.dtype)


def _moba(q, k, km, vt, sg):
    B, S, _ = q.shape
    nb = S // BLK
    hp = MW // LANES
    return pl.pallas_call(
        functools.partial(_moba_kernel, nb=nb),
        grid=(B, hp, nb),
        in_specs=[pl.BlockSpec((1, BLK, LANES), lambda b, p, i: (b, i, p)),
                  pl.BlockSpec((1, S, LANES), lambda b, p, i: (b, 0, p)),
                  pl.BlockSpec((1, nb // PVG, LANES, PVG * BLK), lambda b, p, i: (b, 0, p, 0)),
                  pl.BlockSpec((1, nb, LANES), lambda b, p, i: (b, 0, p)),
                  pl.BlockSpec((1, BLK, LANES), lambda b, p, i: (b, i, p))],
        out_specs=pl.BlockSpec((1, BLK, LANES), lambda b, p, i: (b, i, p)),
        out_shape=jax.ShapeDtypeStruct((B, S, MW), BF16),
        scratch_shapes=[pltpu.VMEM((2, nb, BLK), F32),
                        pltpu.VMEM((3, BLK, BLK), F32),
                        pltpu.VMEM((2, nb, BLK, BLK), F32),
                        pltpu.VMEM((2, nb * BLK, BLK), BF16)],
        compiler_params=pltpu.CompilerParams(
            dimension_semantics=("parallel", "parallel", "arbitrary"), vmem_limit_bytes=VMEM_LIMIT),
        name="moba",
    )(q, k, vt, km, sg)


def _split_bf16(p):
    hi = p.astype(BF16).astype(F32)
    lo = (p - hi).astype(BF16).astype(F32)
    return hi, lo


def _dn_kernel(x_ref, sz_ref, gb_ref, gbt_ref, cw_ref, nw_ref, o_ref, xbuf, qkv, state, mqs, cs, os_, gls):
    t = pl.program_id(1)
    halo = 8

    @pl.when(t == 0)
    def _():
        xbuf[0:halo, :] = jnp.zeros((halo, 3 * DW), F32)
        state[...] = jnp.zeros_like(state)

    @pl.when(t > 0)
    def _():
        xbuf[0:halo, :] = xbuf[TD:TD + halo, :]

    xbuf[halo:halo + TD, :] = x_ref[0]

    for n in range(3 * DH):
        sl = slice(n * DK, (n + 1) * DK)
        y = jnp.zeros((TD, DK), F32)
        for tap in range(CONV):
            lo = halo - (CONV - 1) + tap
            y = y + cw_ref[tap:tap + 1, sl] * xbuf[lo:lo + TD, sl]
        y = _silu(y)
        if n < 2 * DH:
            y = y * lax.rsqrt(jnp.sum(y * y, axis=-1, keepdims=True) + EPS)
        if n < DH:
            y = y * (DK ** -0.5)
        qkv[:, sl] = y

    r = lax.broadcasted_iota(jnp.int32, (CH, CH), 0)
    c = lax.broadcasted_iota(jnp.int32, (CH, CH), 1)
    tril = r >= c
    tril_strict = r > c
    ones_l = tril.astype(F32)
    ones_u = (r <= c).astype(F32)
    nw = nw_ref[...]

    def local(it, _):
        units = [(dc, h) for dc in range(CU) for h in range(DH)]
        ci = [it * CU + dc for dc in range(CU)]
        r0 = [pl.multiple_of(c_ * CH, CH) for c_ in ci]
        gcol = [gb_ref[0, pl.ds(r0[dc], CH), :] for dc in range(CU)]
        grow = [gbt_ref[0, ci[dc]] for dc in range(CU)]
        gc = [jnp.dot(ones_l, g, precision=HI, preferred_element_type=F32) for g in gcol]
        gr = [jnp.dot(g, ones_u, precision=HI, preferred_element_type=F32) for g in grow]
        q = [qkv[pl.ds(r0[dc], CH), h * DK:(h + 1) * DK] for dc, h in units]
        k = [qkv[pl.ds(r0[dc], CH), DW + h * DK:DW + (h + 1) * DK] for dc, h in units]
        v = [qkv[pl.ds(r0[dc], CH), 2 * DW + h * DK:2 * DW + (h + 1) * DK] for dc, h in units]
        beta = [gcol[dc][:, h:h + 1] for dc, h in units]
        gcum = [gc[dc][:, DH + h:DH + h + 1] for dc, h in units]
        glast = [g[CH - 1:CH, :] for g in gcum]
        us = range(len(units))
        decay = [jnp.where(tril, jnp.exp(jnp.where(
            tril, gcum[u] - gr[units[u][0]][DH + units[u][1]:DH + units[u][1] + 1, :], 0.0)), 0.0) for u in us]
        eg = [jnp.exp(g) for g in gcum]
        kb = [k[u] * beta[u] for u in us]
        aq = [lax.dot_general(jnp.concatenate([kb[u], q[u]], axis=0).astype(BF16), k[u].astype(BF16), NT,
                              preferred_element_type=F32) for u in us]
        p = [jnp.where(tril_strict, aq[u][:CH] * decay[u], 0.0) for u in us]
        qk = [aq[u][CH:] * decay[u] for u in us]
        xs = [jnp.concatenate([v[u] * beta[u], kb[u] * eg[u]], axis=1) for u in us]
        for lvl in range(6):
            for u in us:
                ph, plo = _split_bf16(p[u])
                xh, xlo = _split_bf16(xs[u])
                lhs = jnp.concatenate([ph, plo, ph], axis=1).astype(BF16)
                px = jnp.dot(lhs, jnp.concatenate([xh, xh, xlo], axis=0).astype(BF16),
                             preferred_element_type=F32)
                xs[u] = xs[u] - px if lvl == 0 else xs[u] + px
                if lvl < 5:
                    p[u] = jnp.dot(lhs, jnp.concatenate([ph, ph, plo], axis=0).astype(BF16),
                                   preferred_element_type=F32)
        xb = [x.astype(BF16) for x in xs]
        kdt = [(k[u] * jnp.exp(glast[u] - gcum[u])).T.astype(BF16) for u in us]
        cm = [jnp.dot(kdt[u], xb[u], preferred_element_type=F32) for u in us]
        qo = [jnp.dot(qk[u].astype(BF16), xb[u], preferred_element_type=F32) for u in us]
        for dc in range(CU):
            gls[ci[dc]] = jnp.exp(gc[dc][CH - 1:CH, :])
        for u, (dc, h) in enumerate(units):
            mqs[h, ci[dc], 0:DK, :] = cm[u][:, DK:].astype(BF16)
            mqs[h, ci[dc], DK:DK + CH, :] = (q[u] * eg[u] - qo[u][:, DK:]).astype(BF16)
            cs[h, ci[dc]] = cm[u][:, :DK]
            os_[pl.ds(r0[dc], CH), h * DK:(h + 1) * DK] = qo[u][:, :DK]
        return 0

    lax.fori_loop(0, TD // (CH * CU), local, 0)

    def scan(ci, _):
        r0 = pl.multiple_of(ci * CH, CH)
        gl_all = gls[ci]
        for h in range(DH):
            st = state[h]
            rr = jnp.dot(mqs[h, ci], st.astype(BF16), preferred_element_type=F32)
            state[h] = st * gl_all[:, DH + h:DH + h + 1] - rr[:DK] + cs[h, ci]
            o = rr[DK:] + os_[pl.ds(r0, CH), h * DK:(h + 1) * DK]
            o = o * lax.rsqrt(jnp.mean(o * o, axis=-1, keepdims=True) + EPS) * nw
            o = o * sz_ref[0, pl.ds(r0, CH), h * DK:(h + 1) * DK]
            o_ref[0, pl.ds(r0, CH), h * DK:(h + 1) * DK] = o.astype(o_ref.dtype)
        return 0

    lax.fori_loop(0, TD // CH, scan, 0)


def _deltanet(dx, sz, gb, gbt, conv_w, dn_norm_w):
    B, S, _ = dx.shape
    nt = S // TD
    return pl.pallas_call(
        _dn_kernel,
        grid=(B, nt),
        in_specs=[pl.BlockSpec((1, TD, 3 * DW), lambda b, t: (b, t, 0)),
                  pl.BlockSpec((1, TD, DW), lambda b, t: (b, t, 0)),
                  pl.BlockSpec((1, TD, LANES), lambda b, t: (b, t, 0)),
                  pl.BlockSpec((1, TD // CH, 8, CH), lambda b, t: (b, t, 0, 0)),
                  pl.BlockSpec((CONV, 3 * DW), lambda b, t: (0, 0)),
                  pl.BlockSpec((1, DK), lambda b, t: (0, 0))],
        out_specs=pl.BlockSpec((1, TD, DW), lambda b, t: (b, t, 0)),
        out_shape=jax.ShapeDtypeStruct((B, S, DW), BF16),
        scratch_shapes=[pltpu.VMEM((TD + 8, 3 * DW), F32),
                        pltpu.VMEM((TD, 3 * DW), F32),
                        pltpu.VMEM((DH, DK, DK), F32),
                        pltpu.VMEM((DH, TD // CH, DK + CH, DK), BF16),
                        pltpu.VMEM((DH, TD // CH, DK, DK), F32),
                        pltpu.VMEM((TD, DW), F32),
                        pltpu.VMEM((TD // CH, 1, LANES), F32)],
        compiler_params=pltpu.CompilerParams(
            dimension_semantics=("parallel", "arbitrary"), vmem_limit_bytes=VMEM_LIMIT),
        name="deltanet",
    )(dx, sz, gb, gbt, conv_w, dn_norm_w.reshape(1, DK))


def _outproj_kernel(om_ref, od_ref, x_ref, mod_ref, pw_ref, wt_ref, wb_ref, o_ref):
    y = jnp.dot(om_ref[0], wt_ref[...], preferred_element_type=F32)
    y = y + jnp.dot(od_ref[0], wb_ref[...], preferred_element_type=F32)
    yn = y * lax.rsqrt(jnp.mean(y * y, axis=-1, keepdims=True) + EPS) * pw_ref[...]
    o_ref[0] = x_ref[0] + mod_ref[0, 2:3, :] * yn


def _outproj(om, od, x, mod3, post_w, w_out):
    B, S, _ = x.shape
    wb = w_out.astype(BF16)
    row = lambda w: pl.BlockSpec((1, TO, w), lambda b, t: (b, t, 0))
    return pl.pallas_call(
        _outproj_kernel,
        grid=(B, S // TO),
        in_specs=[row(MW), row(DW), row(D),
                  pl.BlockSpec((1, 3, D), lambda b, t: (b, 0, 0)),
                  pl.BlockSpec((1, D), lambda b, t: (0, 0)),
                  pl.BlockSpec((MW, D), lambda b, t: (0, 0)),
                  pl.BlockSpec((DW, D), lambda b, t: (0, 0))],
        out_specs=row(D),
        out_shape=jax.ShapeDtypeStruct((B, S, D), F32),
        compiler_params=pltpu.CompilerParams(
            dimension_semantics=("parallel", "parallel"), vmem_limit_bytes=VMEM_LIMIT),
        name="outproj",
    )(om, od, x, mod3, post_w.reshape(1, D), wb[:MW], wb[MW:])


def kernel(x, c, ada_w, ada_b, pre_norm_w, post_norm_w, w_in, conv_w, a_log, dt_bias, dn_norm_w, w_out):
    B, S, _ = x.shape
    assert S % TD == 0 and S % TO == 0 and S % (BLK * PVG) == 0 and TM == BLK
    depth = ada_w.shape[0]
    for l in range(depth):
        mod3 = _mod(c, ada_w[l], ada_b[l]).reshape(B, 3, D)
        q, k, km, vt, sg, dx, sz, gb, gbt = _inproj(x, mod3, pre_norm_w[l], w_in[l], a_log[l], dt_bias[l])
        om = _moba(q, k, km.reshape(B, S // BLK, MW), vt, sg)
        od = _deltanet(dx, sz, gb, gbt, conv_w[l], dn_norm_w[l])
        x = _outproj(om, od, x, mod3, post_norm_w[l], w_out[l])
    return x
```

```python
import functools
import math

import jax
import jax.numpy as jnp
from jax import lax
from jax.experimental import pallas as pl
from jax.experimental.pallas import tpu as pltpu

F32 = jnp.float32
BF16 = jnp.bfloat16
HI = lax.Precision.HIGHEST

D = 1024
MH, MD = 8, 64
MW = MH * MD
BLK = 256
TOPK = 3
ROPE_D = 16
ROPE_THETA = 500000.0
DH, DK = 4, 128
DW = DH * DK
CONV = 4
CH = 64
EPS = 1e-6
NEG = -1e30
LOWEST = -3.0e38
LOG2E = math.log2(math.e)
PVG = 2

LANES = 128
VMEM_LIMIT = 56 * 1024 * 1024

TM = 256
TD = 512
CU = 2
TO = 512

NT = (((1,), (1,)), ((), ()))
TN = (((0,), (0,)), ((), ()))


def _silu(v):
    return v * jax.nn.sigmoid(v)


def _mod_kernel(c_ref, w_ref, b_ref, o_ref):
    c = c_ref[...]
    o_ref[...] = jnp.dot(_silu(c), w_ref[...], precision=HI, preferred_element_type=F32) + b_ref[...]


def _mod(c, ada_w, ada_b):
    B = c.shape[0]
    tn = 512
    return pl.pallas_call(
        _mod_kernel,
        grid=(3 * D // tn,),
        in_specs=[pl.BlockSpec((B, D), lambda j: (0, 0)),
                  pl.BlockSpec((D, tn), lambda j: (0, j)),
                  pl.BlockSpec((1, tn), lambda j: (0, j))],
        out_specs=pl.BlockSpec((B, tn), lambda j: (0, j)),
        out_shape=jax.ShapeDtypeStruct((B, 3 * D), F32),
        name="mod",
    )(c, ada_w, ada_b.reshape(1, 3 * D))


def _inproj_kernel(x_ref, mod_ref, pw_ref, wqk_ref, wvt_ref, wg_ref, wd_ref, wz_ref, ws_ref,
                   rc_ref, rs1_ref, rs2_ref, alog_ref, dtb_ref,
                   q_ref, k_ref, bias_ref, vt_ref, sg_ref, dx_ref, sz_ref, gb_ref, gbt_ref, kms, *, nb):
    t = pl.program_id(1)
    x = x_ref[0]
    shift = mod_ref[0, 0:1, :]
    scale = mod_ref[0, 1:2, :]
    y = x * lax.rsqrt(jnp.mean(x * x, axis=-1, keepdims=True) + EPS) * pw_ref[...]
    h = (y * (1.0 + scale) + shift).astype(BF16)

    rc, rs1, rs2 = rc_ref[...], rs1_ref[...], rs2_ref[...]

    def rope(t):
        tm1 = pltpu.roll(t, MW - ROPE_D // 2, axis=1)
        tp1 = pltpu.roll(t, ROPE_D // 2, axis=1)
        parts = []
        for n in range(MW // LANES):
            sl = slice(n * LANES, (n + 1) * LANES)
            parts.append(t[:, sl] * rc + tm1[:, sl] * rs1 + tp1[:, sl] * rs2)
        return jnp.concatenate(parts, axis=1)

    qk = jnp.dot(h, wqk_ref[...], preferred_element_type=F32)
    q = rope(qk[:, :MW])
    k = rope(qk[:, MW:])
    k_ref[0] = k.astype(BF16)
    q_ref[0] = (q * (MD ** -0.5 * LOG2E)).astype(BF16)

    @pl.when(t == 0)
    def _():
        kms[...] = jnp.zeros_like(kms)

    km = kms[...]
    wl = lax.broadcasted_iota(jnp.int32, (1, MW), 1)
    kmh = jnp.concatenate([jnp.where((wl >= hh * MD) & (wl < (hh + 1) * MD), km, 0.0) for hh in range(MH)], axis=0)
    gate = lax.dot_general(kmh, q, NT, precision=HI, preferred_element_type=F32)
    blk = lax.broadcasted_iota(jnp.int32, (MH, nb, TM), 1)
    past = blk < t
    g3 = jnp.where(past, gate.reshape(MH, nb, TM), NEG)
    sel = jnp.zeros((MH, nb, TM), jnp.bool_)
    for _ in range(min(TOPK, nb)):
        gmax = jnp.max(g3, axis=1, keepdims=True)
        idx = jnp.min(jnp.where(g3 == gmax, blk, nb), axis=1, keepdims=True)
        pick = blk == idx
        sel = jnp.logical_or(sel, pick)
        g3 = jnp.where(pick, LOWEST, g3)
    bias_ref[0] = jnp.where(jnp.logical_or(jnp.logical_and(sel, past), blk == t), 0.0, NEG)
    kms[pl.ds(t, 1), :] = jnp.mean(k, axis=0, keepdims=True)

    vt = lax.dot_general(wvt_ref[...], h, NT, preferred_element_type=F32)
    vt_ref[0, 0] = vt.astype(BF16)

    sg_ref[0] = _silu(jnp.dot(h, wg_ref[...], preferred_element_type=F32))
    dx_ref[0] = jnp.dot(h, wd_ref[...], preferred_element_type=F32)
    sz_ref[0] = _silu(jnp.dot(h, wz_ref[...], preferred_element_type=F32))

    s = jnp.dot(h, ws_ref[...], preferred_element_type=F32)
    lane = lax.broadcasted_iota(jnp.int32, s.shape, 1)
    beta = jax.nn.sigmoid(s)
    z = s + dtb_ref[...]
    softplus = jnp.maximum(z, 0.0) + jnp.log1p(jnp.exp(-jnp.abs(z)))
    g = -jnp.exp(alog_ref[...]) * softplus
    gb = jnp.where(lane < DH, beta, jnp.where(lane < 2 * DH, g, 0.0))
    gb_ref[0] = gb
    gbt = gb.T[0:8, :]
    for cidx in range(TM // CH):
        gbt_ref[0, cidx] = gbt[:, cidx * CH:(cidx + 1) * CH]


def _inproj(x, mod3, pre_w, w_in, a_log, dt_bias):
    B, S, _ = x.shape
    nb = S // BLK
    nt = S // TM
    wb = w_in.astype(BF16)
    wqk = wb[:, 0:2 * MW]
    wvt = wb[:, 2 * MW:3 * MW].T
    wg = wb[:, 3 * MW:4 * MW]
    wd = wb[:, 4 * MW:4 * MW + 3 * DW]
    wz = wb[:, 4 * MW + 3 * DW:4 * MW + 4 * DW]
    ws = jnp.pad(wb[:, 4 * MW + 4 * DW:], ((0, 0), (0, LANES - 2 * DH)))
    alog = jnp.pad(a_log.astype(F32), (DH, LANES - 2 * DH)).reshape(1, LANES)
    dtb = jnp.pad(dt_bias.astype(F32), (DH, LANES - 2 * DH)).reshape(1, LANES)

    half = ROPE_D // 2
    inv_freq = jnp.power(ROPE_THETA, -jnp.arange(0, ROPE_D, 2, dtype=F32) / ROPE_D)
    ang = jnp.arange(S, dtype=jnp.int32).astype(F32)[:, None] * inv_freq[None, :]
    cos, sin = jnp.cos(ang), jnp.sin(ang)
    one = jnp.ones((S, MD - ROPE_D), F32)
    zero = jnp.zeros((S, MD - ROPE_D), F32)
    zh = jnp.zeros((S, half), F32)
    rc = jnp.tile(jnp.concatenate([cos, cos, one], axis=1), (1, 2))
    rs1 = jnp.tile(jnp.concatenate([-sin, zh, zero], axis=1), (1, 2))
    rs2 = jnp.tile(jnp.concatenate([zh, sin, zero], axis=1), (1, 2))

    const = lambda shape: pl.BlockSpec(shape, lambda b, t: (0,) * len(shape))
    row = lambda w: pl.BlockSpec((1, TM, w), lambda b, t: (b, t, 0))
    tab = pl.BlockSpec((TM, LANES), lambda b, t: (t, 0))
    out_shapes = [
        jax.ShapeDtypeStruct((B, S, MW), BF16),
        jax.ShapeDtypeStruct((B, S, MW), BF16),
        jax.ShapeDtypeStruct((B, MH, nb, S), F32),
        jax.ShapeDtypeStruct((B, nb // PVG, MW, PVG * BLK), BF16),
        jax.ShapeDtypeStruct((B, S, MW), F32),
        jax.ShapeDtypeStruct((B, S, 3 * DW), F32),
        jax.ShapeDtypeStruct((B, S, DW), F32),
        jax.ShapeDtypeStruct((B, S, LANES), F32),
        jax.ShapeDtypeStruct((B, S // CH, 8, CH), F32),
    ]
    out_specs = [
        row(MW), row(MW),
        pl.BlockSpec((1, MH, nb, TM), lambda b, t: (b, 0, 0, t)),
        pl.BlockSpec((1, 1, MW, BLK), lambda b, t: (b, t // PVG, 0, t % PVG)),
        row(MW), row(3 * DW), row(DW), row(LANES),
        pl.BlockSpec((1, TM // CH, 8, CH), lambda b, t: (b, t, 0, 0)),
    ]
    return pl.pallas_call(
        functools.partial(_inproj_kernel, nb=nb),
        grid=(B, nt),
        in_specs=[row(D), pl.BlockSpec((1, 3, D), lambda b, t: (b, 0, 0)), const((1, D)),
                  const((D, 2 * MW)), const((MW, D)), const((D, MW)), const((D, 3 * DW)),
                  const((D, DW)), const((D, LANES)), tab, tab, tab,
                  const((1, LANES)), const((1, LANES))],
        out_specs=out_specs,
        out_shape=out_shapes,
        scratch_shapes=[pltpu.VMEM((nb, MW), F32)],
        compiler_params=pltpu.CompilerParams(
            dimension_semantics=("parallel", "arbitrary"), vmem_limit_bytes=VMEM_LIMIT),
        name="inproj",
    )(x, mod3, pre_w.reshape(1, D), wqk, wvt, wg, wd, wz, ws, rc, rs1, rs2, alog, dtb)


NPAIR = MW // LANES


def _moba_kernel(q_ref, k_ref, vt_ref, bias_ref, sg_ref, o_ref, pm_ref, s0_ref, s1_ref, p0_ref, p1_ref):
    i = pl.program_id(1)
    lane = lax.broadcasted_iota(jnp.int32, (1, LANES), 1)
    head_lanes = [lane < MD, lane >= MD]

    def fold8(a):
        return a.reshape(BLK // 8, 8, BLK)

    @pl.when(i == 0)
    def _():
        kpos = lax.broadcasted_iota(jnp.int32, (BLK, BLK), 0)
        qpos = lax.broadcasted_iota(jnp.int32, (BLK, BLK), 1)
        pm_ref[0] = jnp.zeros((BLK, BLK), F32)
        pm_ref[1] = jnp.where(kpos <= qpos, 0.0, NEG)
        pm_ref[2] = jnp.full((BLK, BLK), NEG, F32)

    ngroups = i // PVG + 1

    s_refs, p_refs = (s0_ref, s1_ref), (p0_ref, p1_ref)

    def scores(pr, qs, j, mx):
        kj = k_ref[0, pl.ds(pl.multiple_of(j * BLK, BLK), BLK), pr * LANES:(pr + 1) * LANES]
        pm = pm_ref[jnp.clip(j - i, -1, 1) + 1]
        out = []
        for h in range(2):
            s = lax.dot_general(kj, qs[h], NT, preferred_element_type=F32) + pm
            s_refs[pr % 2][h, j] = s
            out.append(jnp.maximum(mx[h], jnp.max(fold8(s), axis=0) + bias_ref[0, 2 * pr + h, pl.ds(j, 1), :]))
        return tuple(out)

    def probs(pr, m, j, ls):
        out = []
        for h in range(2):
            p = jnp.exp2(s_refs[pr % 2][h, j] - (m[h] - bias_ref[0, 2 * pr + h, pl.ds(j, 1), :]))
            p_refs[pr % 2][h, pl.ds(pl.multiple_of(j * BLK, BLK), BLK), :] = p.astype(BF16)
            out.append(ls[h] + jnp.sum(fold8(p), axis=0))
        return tuple(out)

    def pv(pr, t, accs):
        r0 = pl.multiple_of(t * (PVG * BLK), PVG * BLK)
        return tuple(
            accs[h] + jnp.dot(vt_ref[0, t, pr * LANES + h * MD:pr * LANES + (h + 1) * MD, :],
                              p_refs[pr % 2][h, pl.ds(r0, PVG * BLK), :], preferred_element_type=F32)
            for h in range(2))

    m, l = {}, {}
    for stage in range(NPAIR + 2):
        p1 = stage if stage < NPAIR else None
        p2 = stage - 1 if 1 <= stage <= NPAIR else None
        p3 = stage - 2 if stage >= 2 else None
        if p1 is not None:
            qp = q_ref[0, :, p1 * LANES:(p1 + 1) * LANES]
            qs = [jnp.where(head_lanes[h], qp, jnp.zeros_like(qp)) for h in range(2)]

        def body(t, carry):
            mx, ls, accs = carry
            if p3 is not None:
                accs = pv(p3, t, accs)
            for dj in range(PVG):
                if p2 is not None:
                    ls = probs(p2, m[p2], t * PVG + dj, ls)
                if p1 is not None:
                    mx = scores(p1, qs, t * PVG + dj, mx)
            return mx, ls, accs

        init = (tuple(jnp.full((8, BLK), NEG, F32) for _ in range(2)) if p1 is not None else (),
                tuple(jnp.zeros((8, BLK), F32) for _ in range(2)) if p2 is not None else (),
                tuple(jnp.zeros((MD, BLK), F32) for _ in range(2)) if p3 is not None else ())
        mx, ls, accs = lax.fori_loop(0, ngroups, body, init)
        if p1 is not None:
            m[p1] = [jnp.max(mx[h], axis=0, keepdims=True) for h in range(2)]
        if p2 is not None:
            l[p2] = [jnp.sum(ls[h], axis=0, keepdims=True) for h in range(2)]
        if p3 is not None:
            ot = jnp.concatenate([accs[h] / l[p3][h] for h in range(2)], axis=0)
            sl = slice(p3 * LANES, (p3 + 1) * LANES)
            o_ref[0, :, sl] = (ot.T * sg_ref[0, :, sl]).astype(o_ref.dtype)


def _moba(q, k, vt, bias, sg):
    B, S, _ = q.shape
    nb = S // BLK
    return pl.pallas_call(
        _moba_kernel,
        grid=(B, nb),
        in_specs=[pl.BlockSpec((1, BLK, MW), lambda b, i: (b, i, 0)),
                  pl.BlockSpec((1, S, MW), lambda b, i: (b, 0, 0)),
                  pl.BlockSpec((1, nb // PVG, MW, PVG * BLK), lambda b, i: (b, 0, 0, 0)),
                  pl.BlockSpec((1, MH, nb, BLK), lambda b, i: (b, 0, 0, i)),
                  pl.BlockSpec((1, BLK, MW), lambda b, i: (b, i, 0))],
        out_specs=pl.BlockSpec((1, BLK, MW), lambda b, i: (b, i, 0)),
        out_shape=jax.ShapeDtypeStruct((B, S, MW), BF16),
        scratch_shapes=[pltpu.VMEM((3, BLK, BLK), F32),
                        pltpu.VMEM((2, nb, BLK, BLK), F32),
                        pltpu.VMEM((2, nb, BLK, BLK), F32),
                        pltpu.VMEM((2, nb * BLK, BLK), BF16),
                        pltpu.VMEM((2, nb * BLK, BLK), BF16)],
        compiler_params=pltpu.CompilerParams(
            dimension_semantics=("parallel", "arbitrary"), vmem_limit_bytes=VMEM_LIMIT),
        name="moba",
    )(q, k, vt, bias, sg)


def _split_bf16(p):
    hi = p.astype(BF16).astype(F32)
    lo = (p - hi).astype(BF16).astype(F32)
    return hi, lo


def _dn_kernel(x_ref, sz_ref, gb_ref, gbt_ref, cw_ref, nw_ref, o_ref, xbuf, qkv, state, mqs, cs, os_, gls):
    t = pl.program_id(1)
    halo = 8

    @pl.when(t == 0)
    def _():
        xbuf[0:halo, :] = jnp.zeros((halo, 3 * DW), F32)
        state[...] = jnp.zeros_like(state)

    @pl.when(t > 0)
    def _():
        xbuf[0:halo, :] = xbuf[TD:TD + halo, :]

    xbuf[halo:halo + TD, :] = x_ref[0]

    for n in range(3 * DH):
        sl = slice(n * DK, (n + 1) * DK)
        y = jnp.zeros((TD, DK), F32)
        for tap in range(CONV):
            lo = halo - (CONV - 1) + tap
            y = y + cw_ref[tap:tap + 1, sl] * xbuf[lo:lo + TD, sl]
        y = _silu(y)
        if n < 2 * DH:
            y = y * lax.rsqrt(jnp.sum(y * y, axis=-1, keepdims=True) + EPS)
        if n < DH:
            y = y * (DK ** -0.5)
        qkv[:, sl] = y

    r = lax.broadcasted_iota(jnp.int32, (CH, CH), 0)
    c = lax.broadcasted_iota(jnp.int32, (CH, CH), 1)
    tril = r >= c
    tril_strict = r > c
    ones_l = tril.astype(F32)
    ones_u = (r <= c).astype(F32)
    nw = nw_ref[...]

    def local(it, _):
        units = [(dc, h) for dc in range(CU) for h in range(DH)]
        ci = [it * CU + dc for dc in range(CU)]
        r0 = [pl.multiple_of(c_ * CH, CH) for c_ in ci]
        gcol = [gb_ref[0, pl.ds(r0[dc], CH), :] for dc in range(CU)]
        grow = [gbt_ref[0, ci[dc]] for dc in range(CU)]
        gc = [jnp.dot(ones_l, g, precision=HI, preferred_element_type=F32) for g in gcol]
        gr = [jnp.dot(g, ones_u, precision=HI, preferred_element_type=F32) for g in grow]
        q = [qkv[pl.ds(r0[dc], CH), h * DK:(h + 1) * DK] for dc, h in units]
        k = [qkv[pl.ds(r0[dc], CH), DW + h * DK:DW + (h + 1) * DK] for dc, h in units]
        v = [qkv[pl.ds(r0[dc], CH), 2 * DW + h * DK:2 * DW + (h + 1) * DK] for dc, h in units]
        beta = [gcol[dc][:, h:h + 1] for dc, h in units]
        gcum = [gc[dc][:, DH + h:DH + h + 1] for dc, h in units]
        glast = [g[CH - 1:CH, :] for g in gcum]
        us = range(len(units))
        decay = [jnp.where(tril, jnp.exp(jnp.where(
            tril, gcum[u] - gr[units[u][0]][DH + units[u][1]:DH + units[u][1] + 1, :], 0.0)), 0.0) for u in us]
        eg = [jnp.exp(g) for g in gcum]
        kb = [k[u] * beta[u] for u in us]
        aq = [lax.dot_general(jnp.concatenate([kb[u], q[u]], axis=0).astype(BF16), k[u].astype(BF16), NT,
                              preferred_element_type=F32) for u in us]
        p = [jnp.where(tril_strict, aq[u][:CH] * decay[u], 0.0) for u in us]
        qk = [aq[u][CH:] * decay[u] for u in us]
        xs = [jnp.concatenate([v[u] * beta[u], kb[u] * eg[u]], axis=1) for u in us]
        for lvl in range(6):
            for u in us:
                ph, plo = _split_bf16(p[u])
                xh, xlo = _split_bf16(xs[u])
                lhs = jnp.concatenate([ph, plo, ph], axis=1).astype(BF16)
                px = jnp.dot(lhs, jnp.concatenate([xh, xh, xlo], axis=0).astype(BF16),
                             preferred_element_type=F32)
                xs[u] = xs[u] - px if lvl == 0 else xs[u] + px
                if lvl < 5:
                    p[u] = jnp.dot(lhs, jnp.concatenate([ph, ph, plo], axis=0).astype(BF16),
                                   preferred_element_type=F32)
        xb = [x.astype(BF16) for x in xs]
        kdt = [(k[u] * jnp.exp(glast[u] - gcum[u])).T.astype(BF16) for u in us]
        cm = [jnp.dot(kdt[u], xb[u], preferred_element_type=F32) for u in us]
        qo = [jnp.dot(qk[u].astype(BF16), xb[u], preferred_element_type=F32) for u in us]
        for dc in range(CU):
            gls[ci[dc]] = jnp.exp(gc[dc][CH - 1:CH, :])
        for u, (dc, h) in enumerate(units):
            mqs[h, ci[dc], 0:DK, :] = cm[u][:, DK:].astype(BF16)
            mqs[h, ci[dc], DK:DK + CH, :] = (q[u] * eg[u] - qo[u][:, DK:]).astype(BF16)
            cs[h, ci[dc]] = cm[u][:, :DK]
            os_[pl.ds(r0[dc], CH), h * DK:(h + 1) * DK] = qo[u][:, :DK]
        return 0

    lax.fori_loop(0, TD // (CH * CU), local, 0)

    def scan(ci, _):
        r0 = pl.multiple_of(ci * CH, CH)
        gl_all = gls[ci]
        for h in range(DH):
            st = state[h]
            rr = jnp.dot(mqs[h, ci], st.astype(BF16), preferred_element_type=F32)
            state[h] = st * gl_all[:, DH + h:DH + h + 1] - rr[:DK] + cs[h, ci]
            o = rr[DK:] + os_[pl.ds(r0, CH), h * DK:(h + 1) * DK]
            o = o * lax.rsqrt(jnp.mean(o * o, axis=-1, keepdims=True) + EPS) * nw
            o = o * sz_ref[0, pl.ds(r0, CH), h * DK:(h + 1) * DK]
            o_ref[0, pl.ds(r0, CH), h * DK:(h + 1) * DK] = o.astype(o_ref.dtype)
        return 0

    lax.fori_loop(0, TD // CH, scan, 0)


def _deltanet(dx, sz, gb, gbt, conv_w, dn_norm_w):
    B, S, _ = dx.shape
    nt = S // TD
    return pl.pallas_call(
        _dn_kernel,
        grid=(B, nt),
        in_specs=[pl.BlockSpec((1, TD, 3 * DW), lambda b, t: (b, t, 0)),
                  pl.BlockSpec((1, TD, DW), lambda b, t: (b, t, 0)),
                  pl.BlockSpec((1, TD, LANES), lambda b, t: (b, t, 0)),
                  pl.BlockSpec((1, TD // CH, 8, CH), lambda b, t: (b, t, 0, 0)),
                  pl.BlockSpec((CONV, 3 * DW), lambda b, t: (0, 0)),
                  pl.BlockSpec((1, DK), lambda b, t: (0, 0))],
        out_specs=pl.BlockSpec((1, TD, DW), lambda b, t: (b, t, 0)),
        out_shape=jax.ShapeDtypeStruct((B, S, DW), BF16),
        scratch_shapes=[pltpu.VMEM((TD + 8, 3 * DW), F32),
                        pltpu.VMEM((TD, 3 * DW), F32),
                        pltpu.VMEM((DH, DK, DK), F32),
                        pltpu.VMEM((DH, TD // CH, DK + CH, DK), BF16),
                        pltpu.VMEM((DH, TD // CH, DK, DK), F32),
                        pltpu.VMEM((TD, DW), F32),
                        pltpu.VMEM((TD // CH, 1, LANES), F32)],
        compiler_params=pltpu.CompilerParams(
            dimension_semantics=("parallel", "arbitrary"), vmem_limit_bytes=VMEM_LIMIT),
        name="deltanet",
    )(dx, sz, gb, gbt, conv_w, dn_norm_w.reshape(1, DK))


def _outproj_kernel(om_ref, od_ref, x_ref, mod_ref, pw_ref, wt_ref, wb_ref, o_ref):
    y = jnp.dot(om_ref[0], wt_ref[...], preferred_element_type=F32)
    y = y + jnp.dot(od_ref[0], wb_ref[...], preferred_element_type=F32)
    yn = y * lax.rsqrt(jnp.mean(y * y, axis=-1, keepdims=True) + EPS) * pw_ref[...]
    o_ref[0] = x_ref[0] + mod_ref[0, 2:3, :] * yn


def _outproj(om, od, x, mod3, post_w, w_out):
    B, S, _ = x.shape
    wb = w_out.astype(BF16)
    row = lambda w: pl.BlockSpec((1, TO, w), lambda b, t: (b, t, 0))
    return pl.pallas_call(
        _outproj_kernel,
        grid=(B, S // TO),
        in_specs=[row(MW), row(DW), row(D),
                  pl.BlockSpec((1, 3, D), lambda b, t: (b, 0, 0)),
                  pl.BlockSpec((1, D), lambda b, t: (0, 0)),
                  pl.BlockSpec((MW, D), lambda b, t: (0, 0)),
                  pl.BlockSpec((DW, D), lambda b, t: (0, 0))],
        out_specs=row(D),
        out_shape=jax.ShapeDtypeStruct((B, S, D), F32),
        compiler_params=pltpu.CompilerParams(
            dimension_semantics=("parallel", "parallel"), vmem_limit_bytes=VMEM_LIMIT),
        name="outproj",
    )(om, od, x, mod3, post_w.reshape(1, D), wb[:MW], wb[MW:])


def kernel(x, c, ada_w, ada_b, pre_norm_w, post_norm_w, w_in, conv_w, a_log, dt_bias, dn_norm_w, w_out):
    B, S, _ = x.shape
    assert S % TD == 0 and S % TO == 0 and S % (BLK * PVG) == 0 and TM == BLK
    depth = ada_w.shape[0]
    for l in range(depth):
        mod3 = _mod(c, ada_w[l], ada_b[l]).reshape(B, 3, D)
        q, k, bias, vt, sg, dx, sz, gb, gbt = _inproj(x, mod3, pre_norm_w[l], w_in[l], a_log[l], dt_bias[l])
        om = _moba(q, k, vt, bias, sg)
        od = _deltanet(dx, sz, gb, gbt, conv_w[l], dn_norm_w[l])
        x = _outproj(om, od, x, mod3, post_norm_w[l], w_out[l])
    return x
```

```python
import functools
import math

import jax
import jax.numpy as jnp
from jax import lax
from jax.experimental import pallas as pl
from jax.experimental.pallas import tpu as pltpu

F32 = jnp.float32
BF16 = jnp.bfloat16
HI = lax.Precision.HIGHEST

D = 1024
MH, MD = 8, 64
MW = MH * MD
BLK = 256
TOPK = 3
ROPE_D = 16
ROPE_THETA = 500000.0
DH, DK = 4, 128
DW = DH * DK
CONV = 4
CH = 64
EPS = 1e-6
NEG = -1e30
LOWEST = -3.0e38
LOG2E = math.log2(math.e)
PVG = 2

LANES = 128
VMEM_LIMIT = 56 * 1024 * 1024

TM = 256
TD = 512
CU = 2
TO = 512

NT = (((1,), (1,)), ((), ()))
TN = (((0,), (0,)), ((), ()))


def _silu(v):
    return v * jax.nn.sigmoid(v)


def _mod_kernel(c_ref, w_ref, b_ref, o_ref):
    c = c_ref[...]
    o_ref[...] = jnp.dot(_silu(c), w_ref[...], precision=HI, preferred_element_type=F32) + b_ref[...]


def _mod(c, ada_w, ada_b):
    B = c.shape[0]
    tn = 512
    return pl.pallas_call(
        _mod_kernel,
        grid=(3 * D // tn,),
        in_specs=[pl.BlockSpec((B, D), lambda j: (0, 0)),
                  pl.BlockSpec((D, tn), lambda j: (0, j)),
                  pl.BlockSpec((1, tn), lambda j: (0, j))],
        out_specs=pl.BlockSpec((B, tn), lambda j: (0, j)),
        out_shape=jax.ShapeDtypeStruct((B, 3 * D), F32),
        name="mod",
    )(c, ada_w, ada_b.reshape(1, 3 * D))


def _inproj_kernel(x_ref, mod_ref, pw_ref, wqk_ref, wvt_ref, wg_ref, wd_ref, wz_ref, ws_ref,
                   rc_ref, rs1_ref, rs2_ref, alog_ref, dtb_ref,
                   q_ref, k_ref, bias_ref, vt_ref, sg_ref, dx_ref, sz_ref, gb_ref, gbt_ref, kms, *, nb):
    t = pl.program_id(1)
    x = x_ref[0]
    shift = mod_ref[0, 0:1, :]
    scale = mod_ref[0, 1:2, :]
    y = x * lax.rsqrt(jnp.mean(x * x, axis=-1, keepdims=True) + EPS) * pw_ref[...]
    h = (y * (1.0 + scale) + shift).astype(BF16)

    rc, rs1, rs2 = rc_ref[...], rs1_ref[...], rs2_ref[...]

    def rope(t):
        tm1 = pltpu.roll(t, MW - ROPE_D // 2, axis=1)
        tp1 = pltpu.roll(t, ROPE_D // 2, axis=1)
        parts = []
        for n in range(MW // LANES):
            sl = slice(n * LANES, (n + 1) * LANES)
            parts.append(t[:, sl] * rc + tm1[:, sl] * rs1 + tp1[:, sl] * rs2)
        return jnp.concatenate(parts, axis=1)

    qk = jnp.dot(h, wqk_ref[...], preferred_element_type=F32)
    q = rope(qk[:, :MW])
    k = rope(qk[:, MW:])
    k_ref[0] = k.astype(BF16)
    q_ref[0] = (q * (MD ** -0.5 * LOG2E)).astype(BF16)

    @pl.when(t == 0)
    def _():
        kms[...] = jnp.zeros_like(kms)

    km = kms[...]
    wl = lax.broadcasted_iota(jnp.int32, (1, MW), 1)
    kmh = jnp.concatenate([jnp.where((wl >= hh * MD) & (wl < (hh + 1) * MD), km, 0.0) for hh in range(MH)], axis=0)
    gate = lax.dot_general(kmh, q, NT, precision=HI, preferred_element_type=F32)
    blk = lax.broadcasted_iota(jnp.int32, (MH, nb, TM), 1)
    past = blk < t
    g3 = jnp.where(past, gate.reshape(MH, nb, TM), NEG)
    sel = jnp.zeros((MH, nb, TM), jnp.bool_)
    for _ in range(min(TOPK, nb)):
        gmax = jnp.max(g3, axis=1, keepdims=True)
        idx = jnp.min(jnp.where(g3 == gmax, blk, nb), axis=1, keepdims=True)
        pick = blk == idx
        sel = jnp.logical_or(sel, pick)
        g3 = jnp.where(pick, LOWEST, g3)
    bias_ref[0] = jnp.where(jnp.logical_or(jnp.logical_and(sel, past), blk == t), 0.0, NEG)
    kms[pl.ds(t, 1), :] = jnp.mean(k, axis=0, keepdims=True)

    vt = lax.dot_general(wvt_ref[...], h, NT, preferred_element_type=F32)
    vt_ref[0, 0] = vt.astype(BF16)

    sg_ref[0] = _silu(jnp.dot(h, wg_ref[...], preferred_element_type=F32))
    dx_ref[0] = jnp.dot(h, wd_ref[...], preferred_element_type=F32)
    sz_ref[0] = _silu(jnp.dot(h, wz_ref[...], preferred_element_type=F32))

    s = jnp.dot(h, ws_ref[...], preferred_element_type=F32)
    lane = lax.broadcasted_iota(jnp.int32, s.shape, 1)
    beta = jax.nn.sigmoid(s)
    z = s + dtb_ref[...]
    softplus = jnp.maximum(z, 0.0) + jnp.log1p(jnp.exp(-jnp.abs(z)))
    g = -jnp.exp(alog_ref[...]) * softplus
    gb = jnp.where(lane < DH, beta, jnp.where(lane < 2 * DH, g, 0.0))
    gb_ref[0] = gb
    gbt = gb.T[0:8, :]
    for cidx in range(TM // CH):
        gbt_ref[0, cidx] = gbt[:, cidx * CH:(cidx + 1) * CH]


def _inproj(x, mod3, pre_w, w_in, a_log, dt_bias):
    B, S, _ = x.shape
    nb = S // BLK
    nt = S // TM
    wb = w_in.astype(BF16)
    wqk = wb[:, 0:2 * MW]
    wvt = wb[:, 2 * MW:3 * MW].T
    wg = wb[:, 3 * MW:4 * MW]
    wd = wb[:, 4 * MW:4 * MW + 3 * DW]
    wz = wb[:, 4 * MW + 3 * DW:4 * MW + 4 * DW]
    ws = jnp.pad(wb[:, 4 * MW + 4 * DW:], ((0, 0), (0, LANES - 2 * DH)))
    alog = jnp.pad(a_log.astype(F32), (DH, LANES - 2 * DH)).reshape(1, LANES)
    dtb = jnp.pad(dt_bias.astype(F32), (DH, LANES - 2 * DH)).reshape(1, LANES)

    half = ROPE_D // 2
    inv_freq = jnp.power(ROPE_THETA, -jnp.arange(0, ROPE_D, 2, dtype=F32) / ROPE_D)
    ang = jnp.arange(S, dtype=jnp.int32).astype(F32)[:, None] * inv_freq[None, :]
    cos, sin = jnp.cos(ang), jnp.sin(ang)
    one = jnp.ones((S, MD - ROPE_D), F32)
    zero = jnp.zeros((S, MD - ROPE_D), F32)
    zh = jnp.zeros((S, half), F32)
    rc = jnp.tile(jnp.concatenate([cos, cos, one], axis=1), (1, 2))
    rs1 = jnp.tile(jnp.concatenate([-sin, zh, zero], axis=1), (1, 2))
    rs2 = jnp.tile(jnp.concatenate([zh, sin, zero], axis=1), (1, 2))

    const = lambda shape: pl.BlockSpec(shape, lambda b, t: (0,) * len(shape))
    row = lambda w: pl.BlockSpec((1, TM, w), lambda b, t: (b, t, 0))
    tab = pl.BlockSpec((TM, LANES), lambda b, t: (t, 0))
    out_shapes = [
        jax.ShapeDtypeStruct((B, S, MW), BF16),
        jax.ShapeDtypeStruct((B, S, MW), BF16),
        jax.ShapeDtypeStruct((B, MH, nb, S), F32),
        jax.ShapeDtypeStruct((B, nb, MW, BLK), BF16),
        jax.ShapeDtypeStruct((B, S, MW), F32),
        jax.ShapeDtypeStruct((B, S, 3 * DW), F32),
        jax.ShapeDtypeStruct((B, S, DW), F32),
        jax.ShapeDtypeStruct((B, S, LANES), F32),
        jax.ShapeDtypeStruct((B, S // CH, 8, CH), F32),
    ]
    out_specs = [
        row(MW), row(MW),
        pl.BlockSpec((1, MH, nb, TM), lambda b, t: (b, 0, 0, t)),
        pl.BlockSpec((1, 1, MW, BLK), lambda b, t: (b, t, 0, 0)),
        row(MW), row(3 * DW), row(DW), row(LANES),
        pl.BlockSpec((1, TM // CH, 8, CH), lambda b, t: (b, t, 0, 0)),
    ]
    return pl.pallas_call(
        functools.partial(_inproj_kernel, nb=nb),
        grid=(B, nt),
        in_specs=[row(D), pl.BlockSpec((1, 3, D), lambda b, t: (b, 0, 0)), const((1, D)),
                  const((D, 2 * MW)), const((MW, D)), const((D, MW)), const((D, 3 * DW)),
                  const((D, DW)), const((D, LANES)), tab, tab, tab,
                  const((1, LANES)), const((1, LANES))],
        out_specs=out_specs,
        out_shape=out_shapes,
        scratch_shapes=[pltpu.VMEM((nb, MW), F32)],
        compiler_params=pltpu.CompilerParams(
            dimension_semantics=("parallel", "arbitrary"), vmem_limit_bytes=VMEM_LIMIT),
        name="inproj",
    )(x, mod3, pre_w.reshape(1, D), wqk, wvt, wg, wd, wz, ws, rc, rs1, rs2, alog, dtb)


NPAIR = MW // LANES
ONES = 16


def _moba_kernel(q_ref, k_ref, vt_ref, bias_ref, sg_ref, o_ref,
                 pm_ref, s0_ref, s1_ref, qm_ref, acc_ref):
    i = pl.program_id(1)
    lane = lax.broadcasted_iota(jnp.int32, (1, LANES), 1)
    head_lanes = [lane < MD, lane >= MD]

    def fold8(a):
        return a.reshape(BLK // 8, 8, BLK)

    @pl.when(i == 0)
    def _():
        kpos = lax.broadcasted_iota(jnp.int32, (BLK, BLK), 0)
        qpos = lax.broadcasted_iota(jnp.int32, (BLK, BLK), 1)
        pm_ref[0] = jnp.zeros((BLK, BLK), F32)
        pm_ref[1] = jnp.where(kpos <= qpos, 0.0, NEG)
        pm_ref[2] = jnp.full((BLK, BLK), NEG, F32)

    ngroups = i // PVG + 1

    s_refs = (s0_ref, s1_ref)

    def scores(pr, qs, j, mx):
        kj = k_ref[0, pl.ds(pl.multiple_of(j * BLK, BLK), BLK), pr * LANES:(pr + 1) * LANES]
        pm = pm_ref[jnp.clip(j - i, -1, 1) + 1]
        out = []
        for h in range(2):
            s = lax.dot_general(kj, qs[h][...], NT, preferred_element_type=F32) + pm
            s_refs[pr % 2][h, j] = s
            out.append(jnp.maximum(mx[h], jnp.max(fold8(s), axis=0) + bias_ref[0, 2 * pr + h, pl.ds(j, 1), :]))
        return tuple(out)

    ones_rows = jnp.ones((ONES, BLK), BF16)

    def probs_pv(pr, m, j):
        for h in range(2):
            p = jnp.exp2(s_refs[pr % 2][h, j] - (m[h] - bias_ref[0, 2 * pr + h, pl.ds(j, 1), :]))
            vth = vt_ref[0, j, pr * LANES + h * MD:pr * LANES + (h + 1) * MD, :]
            acc_ref[h] += jnp.dot(jnp.concatenate([vth, ones_rows], axis=0), p.astype(BF16),
                                  preferred_element_type=F32)

    m = {}
    for stage in range(NPAIR + 1):
        p1 = stage if stage < NPAIR else None
        p2 = stage - 1 if stage >= 1 else None
        qs = None
        if p1 is not None:
            qp = q_ref[0, :, p1 * LANES:(p1 + 1) * LANES]
            for h in range(2):
                qm_ref[p1 % 2, h] = jnp.where(head_lanes[h], qp, jnp.zeros_like(qp))
            qs = [qm_ref.at[p1 % 2, h] for h in range(2)]
        if p2 is not None:
            acc_ref[...] = jnp.zeros_like(acc_ref)

        def body(t, mx):
            for dj in range(PVG):
                if p2 is not None:
                    probs_pv(p2, m[p2], t * PVG + dj)
                if p1 is not None:
                    mx = scores(p1, qs, t * PVG + dj, mx)
            return mx

        init = tuple(jnp.full((8, BLK), NEG, F32) for _ in range(2)) if p1 is not None else ()
        mx = lax.fori_loop(0, ngroups, body, init)
        if p1 is not None:
            m[p1] = [jnp.max(mx[h], axis=0, keepdims=True) for h in range(2)]
        if p2 is not None:
            ot = jnp.concatenate([acc_ref[h, 0:MD, :] / acc_ref[h, MD:MD + 1, :] for h in range(2)],
                                 axis=0)
            sl = slice(p2 * LANES, (p2 + 1) * LANES)
            o_ref[0, :, sl] = (ot.T * sg_ref[0, :, sl]).astype(o_ref.dtype)


def _moba(q, k, vt, bias, sg):
    B, S, _ = q.shape
    nb = S // BLK
    return pl.pallas_call(
        _moba_kernel,
        grid=(B, nb),
        in_specs=[pl.BlockSpec((1, BLK, MW), lambda b, i: (b, i, 0)),
                  pl.BlockSpec((1, S, MW), lambda b, i: (b, 0, 0)),
                  pl.BlockSpec((1, nb, MW, BLK), lambda b, i: (b, 0, 0, 0)),
                  pl.BlockSpec((1, MH, nb, BLK), lambda b, i: (b, 0, 0, i)),
                  pl.BlockSpec((1, BLK, MW), lambda b, i: (b, i, 0))],
        out_specs=pl.BlockSpec((1, BLK, MW), lambda b, i: (b, i, 0)),
        out_shape=jax.ShapeDtypeStruct((B, S, MW), BF16),
        scratch_shapes=[pltpu.VMEM((3, BLK, BLK), F32),
                        pltpu.VMEM((2, nb, BLK, BLK), F32),
                        pltpu.VMEM((2, nb, BLK, BLK), F32),
                        pltpu.VMEM((2, 2, BLK, LANES), BF16),
                        pltpu.VMEM((2, MD + ONES, BLK), F32)],
        compiler_params=pltpu.CompilerParams(
            dimension_semantics=("parallel", "arbitrary"), vmem_limit_bytes=VMEM_LIMIT),
        name="moba",
    )(q, k, vt, bias, sg)


def _split_bf16(p):
    hi = p.astype(BF16).astype(F32)
    lo = (p - hi).astype(BF16).astype(F32)
    return hi, lo


def _dn_kernel(x_ref, sz_ref, gb_ref, gbt_ref, cw_ref, nw_ref, o_ref, xbuf, qkv, state, mqs, cs, os_, gls):
    t = pl.program_id(1)
    halo = 8

    @pl.when(t == 0)
    def _():
        xbuf[0:halo, :] = jnp.zeros((halo, 3 * DW), F32)
        state[...] = jnp.zeros_like(state)

    @pl.when(t > 0)
    def _():
        xbuf[0:halo, :] = xbuf[TD:TD + halo, :]

    xbuf[halo:halo + TD, :] = x_ref[0]

    for n in range(3 * DH):
        sl = slice(n * DK, (n + 1) * DK)
        y = jnp.zeros((TD, DK), F32)
        for tap in range(CONV):
            lo = halo - (CONV - 1) + tap
            y = y + cw_ref[tap:tap + 1, sl] * xbuf[lo:lo + TD, sl]
        y = _silu(y)
        if n < 2 * DH:
            y = y * lax.rsqrt(jnp.sum(y * y, axis=-1, keepdims=True) + EPS)
        if n < DH:
            y = y * (DK ** -0.5)
        qkv[:, sl] = y

    r = lax.broadcasted_iota(jnp.int32, (CH, CH), 0)
    c = lax.broadcasted_iota(jnp.int32, (CH, CH), 1)
    tril = r >= c
    tril_strict = r > c
    ones_l = tril.astype(F32)
    ones_u = (r <= c).astype(F32)
    nw = nw_ref[...]

    def local(it, _):
        units = [(dc, h) for dc in range(CU) for h in range(DH)]
        ci = [it * CU + dc for dc in range(CU)]
        r0 = [pl.multiple_of(c_ * CH, CH) for c_ in ci]
        gcol = [gb_ref[0, pl.ds(r0[dc], CH), :] for dc in range(CU)]
        grow = [gbt_ref[0, ci[dc]] for dc in range(CU)]
        gc = [jnp.dot(ones_l, g, precision=HI, preferred_element_type=F32) for g in gcol]
        gr = [jnp.dot(g, ones_u, precision=HI, preferred_element_type=F32) for g in grow]
        q = [qkv[pl.ds(r0[dc], CH), h * DK:(h + 1) * DK] for dc, h in units]
        k = [qkv[pl.ds(r0[dc], CH), DW + h * DK:DW + (h + 1) * DK] for dc, h in units]
        v = [qkv[pl.ds(r0[dc], CH), 2 * DW + h * DK:2 * DW + (h + 1) * DK] for dc, h in units]
        beta = [gcol[dc][:, h:h + 1] for dc, h in units]
        gcum = [gc[dc][:, DH + h:DH + h + 1] for dc, h in units]
        glast = [g[CH - 1:CH, :] for g in gcum]
        us = range(len(units))
        decay = [jnp.where(tril, jnp.exp(jnp.where(
            tril, gcum[u] - gr[units[u][0]][DH + units[u][1]:DH + units[u][1] + 1, :], 0.0)), 0.0) for u in us]
        eg = [jnp.exp(g) for g in gcum]
        kb = [k[u] * beta[u] for u in us]
        aq = [lax.dot_general(jnp.concatenate([kb[u], q[u]], axis=0).astype(BF16), k[u].astype(BF16), NT,
                              preferred_element_type=F32) for u in us]
        p = [jnp.where(tril_strict, aq[u][:CH] * decay[u], 0.0) for u in us]
        qk = [aq[u][CH:] * decay[u] for u in us]
        xs = [jnp.concatenate([v[u] * beta[u], kb[u] * eg[u]], axis=1) for u in us]
        for lvl in range(6):
            for u in us:
                ph, plo = _split_bf16(p[u])
                xh, xlo = _split_bf16(xs[u])
                lhs = jnp.concatenate([ph, plo, ph], axis=1).astype(BF16)
                px = jnp.dot(lhs, jnp.concatenate([xh, xh, xlo], axis=0).astype(BF16),
                             preferred_element_type=F32)
                xs[u] = xs[u] - px if lvl == 0 else xs[u] + px
                if lvl < 5:
                    p[u] = jnp.dot(lhs, jnp.concatenate([ph, ph, plo], axis=0).astype(BF16),
                                   preferred_element_type=F32)
        xb = [x.astype(BF16) for x in xs]
        kdt = [(k[u] * jnp.exp(glast[u] - gcum[u])).T.astype(BF16) for u in us]
        cm = [jnp.dot(kdt[u], xb[u], preferred_element_type=F32) for u in us]
        qo = [jnp.dot(qk[u].astype(BF16), xb[u], preferred_element_type=F32) for u in us]
        for dc in range(CU):
            gls[ci[dc]] = jnp.exp(gc[dc][CH - 1:CH, :])
        for u, (dc, h) in enumerate(units):
            mqs[h, ci[dc], 0:DK, :] = cm[u][:, DK:].astype(BF16)
            mqs[h, ci[dc], DK:DK + CH, :] = (q[u] * eg[u] - qo[u][:, DK:]).astype(BF16)
            cs[h, ci[dc]] = cm[u][:, :DK]
            os_[pl.ds(r0[dc], CH), h * DK:(h + 1) * DK] = qo[u][:, :DK]
        return 0

    lax.fori_loop(0, TD // (CH * CU), local, 0)

    def scan(ci, _):
        r0 = pl.multiple_of(ci * CH, CH)
        gl_all = gls[ci]
        for h in range(DH):
            st = state[h]
            rr = jnp.dot(mqs[h, ci], st.astype(BF16), preferred_element_type=F32)
            state[h] = st * gl_all[:, DH + h:DH + h + 1] - rr[:DK] + cs[h, ci]
            o = rr[DK:] + os_[pl.ds(r0, CH), h * DK:(h + 1) * DK]
            o = o * lax.rsqrt(jnp.mean(o * o, axis=-1, keepdims=True) + EPS) * nw
            o = o * sz_ref[0, pl.ds(r0, CH), h * DK:(h + 1) * DK]
            o_ref[0, pl.ds(r0, CH), h * DK:(h + 1) * DK] = o.astype(o_ref.dtype)
        return 0

    lax.fori_loop(0, TD // CH, scan, 0)


def _deltanet(dx, sz, gb, gbt, conv_w, dn_norm_w):
    B, S, _ = dx.shape
    nt = S // TD
    return pl.pallas_call(
        _dn_kernel,
        grid=(B, nt),
        in_specs=[pl.BlockSpec((1, TD, 3 * DW), lambda b, t: (b, t, 0)),
                  pl.BlockSpec((1, TD, DW), lambda b, t: (b, t, 0)),
                  pl.BlockSpec((1, TD, LANES), lambda b, t: (b, t, 0)),
                  pl.BlockSpec((1, TD // CH, 8, CH), lambda b, t: (b, t, 0, 0)),
                  pl.BlockSpec((CONV, 3 * DW), lambda b, t: (0, 0)),
                  pl.BlockSpec((1, DK), lambda b, t: (0, 0))],
        out_specs=pl.BlockSpec((1, TD, DW), lambda b, t: (b, t, 0)),
        out_shape=jax.ShapeDtypeStruct((B, S, DW), BF16),
        scratch_shapes=[pltpu.VMEM((TD + 8, 3 * DW), F32),
                        pltpu.VMEM((TD, 3 * DW), F32),
                        pltpu.VMEM((DH, DK, DK), F32),
                        pltpu.VMEM((DH, TD // CH, DK + CH, DK), BF16),
                        pltpu.VMEM((DH, TD // CH, DK, DK), F32),
                        pltpu.VMEM((TD, DW), F32),
                        pltpu.VMEM((TD // CH, 1, LANES), F32)],
        compiler_params=pltpu.CompilerParams(
            dimension_semantics=("parallel", "arbitrary"), vmem_limit_bytes=VMEM_LIMIT),
        name="deltanet",
    )(dx, sz, gb, gbt, conv_w, dn_norm_w.reshape(1, DK))


def _outproj_kernel(om_ref, od_ref, x_ref, mod_ref, pw_ref, wt_ref, wb_ref, o_ref):
    y = jnp.dot(om_ref[0], wt_ref[...], preferred_element_type=F32)
    y = y + jnp.dot(od_ref[0], wb_ref[...], preferred_element_type=F32)
    yn = y * lax.rsqrt(jnp.mean(y * y, axis=-1, keepdims=True) + EPS) * pw_ref[...]
    o_ref[0] = x_ref[0] + mod_ref[0, 2:3, :] * yn


def _outproj(om, od, x, mod3, post_w, w_out):
    B, S, _ = x.shape
    wb = w_out.astype(BF16)
    row = lambda w: pl.BlockSpec((1, TO, w), lambda b, t: (b, t, 0))
    return pl.pallas_call(
        _outproj_kernel,
        grid=(B, S // TO),
        in_specs=[row(MW), row(DW), row(D),
                  pl.BlockSpec((1, 3, D), lambda b, t: (b, 0, 0)),
                  pl.BlockSpec((1, D), lambda b, t: (0, 0)),
                  pl.BlockSpec((MW, D), lambda b, t: (0, 0)),
                  pl.BlockSpec((DW, D), lambda b, t: (0, 0))],
        out_specs=row(D),
        out_shape=jax.ShapeDtypeStruct((B, S, D), F32),
        compiler_params=pltpu.CompilerParams(
            dimension_semantics=("parallel", "parallel"), vmem_limit_bytes=VMEM_LIMIT),
        name="outproj",
    )(om, od, x, mod3, post_w.reshape(1, D), wb[:MW], wb[MW:])


def kernel(x, c, ada_w, ada_b, pre_norm_w, post_norm_w, w_in, conv_w, a_log, dt_bias, dn_norm_w, w_out):
    B, S, _ = x.shape
    assert S % TD == 0 and S % TO == 0 and S % (BLK * PVG) == 0 and TM == BLK
    depth = ada_w.shape[0]
    for l in range(depth):
        mod3 = _mod(c, ada_w[l], ada_b[l]).reshape(B, 3, D)
        q, k, bias, vt, sg, dx, sz, gb, gbt = _inproj(x, mod3, pre_norm_w[l], w_in[l], a_log[l], dt_bias[l])
        om = _moba(q, k, vt, bias, sg)
        od = _deltanet(dx, sz, gb, gbt, conv_w[l], dn_norm_w[l])
        x = _outproj(om, od, x, mod3, post_norm_w[l], w_out[l])
    return x
```

```python
import functools
import math

import jax
import jax.numpy as jnp
from jax import lax
from jax.experimental import pallas as pl
from jax.experimental.pallas import tpu as pltpu

F32 = jnp.float32
BF16 = jnp.bfloat16
HI = lax.Precision.HIGHEST

D = 1024
MH, MD = 8, 64
MW = MH * MD
BLK = 256
TOPK = 3
ROPE_D = 16
ROPE_THETA = 500000.0
DH, DK = 4, 128
DW = DH * DK
CONV = 4
CH = 64
EPS = 1e-6
NEG = -1e30
LOWEST = -3.0e38
LOG2E = math.log2(math.e)
PVG = 4

LANES = 128
VMEM_LIMIT = 56 * 1024 * 1024

TM = 256
TD = 512
CU = 2
TO = 512

NT = (((1,), (1,)), ((), ()))
TN = (((0,), (0,)), ((), ()))


def _silu(v):
    return v * jax.nn.sigmoid(v)


def _split_bf16(p):
    hi = p.astype(BF16).astype(F32)
    lo = (p - hi).astype(BF16).astype(F32)
    return hi, lo


def _split3_bf16(p):
    a, r = _split_bf16(p)
    c = (p - a - r).astype(BF16).astype(F32)
    return a, r, c


def _mod_kernel(c_ref, w_ref, b_ref, o_ref):
    c = c_ref[...]
    o_ref[...] = jnp.dot(_silu(c), w_ref[...], precision=HI, preferred_element_type=F32) + b_ref[...]


def _mod(c, ada_w, ada_b):
    B = c.shape[0]
    tn = 512
    return pl.pallas_call(
        _mod_kernel,
        grid=(3 * D // tn,),
        in_specs=[pl.BlockSpec((B, D), lambda j: (0, 0)),
                  pl.BlockSpec((D, tn), lambda j: (0, j)),
                  pl.BlockSpec((1, tn), lambda j: (0, j))],
        out_specs=pl.BlockSpec((B, tn), lambda j: (0, j)),
        out_shape=jax.ShapeDtypeStruct((B, 3 * D), F32),
        name="mod",
    )(c, ada_w, ada_b.reshape(1, 3 * D))


def _inproj_kernel(x_ref, mod_ref, pw_ref, wqk_ref, wvt_ref, wg_ref, wd_ref, wz_ref, ws_ref,
                   rc_ref, rs1_ref, rs2_ref, alog_ref, dtb_ref,
                   q_ref, k_ref, bias_ref, vt_ref, sg_ref, dx_ref, sz_ref, gb_ref, gbt_ref, kms, *, nb):
    t = pl.program_id(1)
    x = x_ref[0]
    shift = mod_ref[0, 0:1, :]
    scale = mod_ref[0, 1:2, :]
    y = x * lax.rsqrt(jnp.mean(x * x, axis=-1, keepdims=True) + EPS) * pw_ref[...]
    h = (y * (1.0 + scale) + shift).astype(BF16)

    rc, rs1, rs2 = rc_ref[...], rs1_ref[...], rs2_ref[...]

    def rope(t):
        tm1 = pltpu.roll(t, MW - ROPE_D // 2, axis=1)
        tp1 = pltpu.roll(t, ROPE_D // 2, axis=1)
        parts = []
        for n in range(MW // LANES):
            sl = slice(n * LANES, (n + 1) * LANES)
            parts.append(t[:, sl] * rc + tm1[:, sl] * rs1 + tp1[:, sl] * rs2)
        return jnp.concatenate(parts, axis=1)

    qk = jnp.dot(h, wqk_ref[...], preferred_element_type=F32)
    q = rope(qk[:, :MW])
    k = rope(qk[:, MW:])
    k_ref[0] = k.astype(BF16)
    q_ref[0] = (q * (MD ** -0.5 * LOG2E)).astype(BF16)

    @pl.when(t == 0)
    def _():
        kms[...] = jnp.zeros_like(kms)

    km = kms[...]
    wl = lax.broadcasted_iota(jnp.int32, (1, MW), 1)
    kmh = jnp.concatenate([jnp.where((wl >= hh * MD) & (wl < (hh + 1) * MD), km, 0.0) for hh in range(MH)], axis=0)
    kh, kl = _split_bf16(kmh)
    qh, ql = _split_bf16(q)
    gate = lax.dot_general(jnp.concatenate([kh, kl, kh], axis=1).astype(BF16),
                           jnp.concatenate([qh, qh, ql], axis=1).astype(BF16), NT,
                           preferred_element_type=F32)
    blk = lax.broadcasted_iota(jnp.int32, (MH, nb, TM), 1)
    past = blk < t
    g3 = jnp.where(past, gate.reshape(MH, nb, TM), NEG)
    sel = jnp.zeros((MH, nb, TM), jnp.bool_)
    for _ in range(min(TOPK, nb)):
        gmax = jnp.max(g3, axis=1, keepdims=True)
        idx = jnp.min(jnp.where(g3 == gmax, blk, nb), axis=1, keepdims=True)
        pick = blk == idx
        sel = jnp.logical_or(sel, pick)
        g3 = jnp.where(pick, LOWEST, g3)
    bias_ref[0] = jnp.where(jnp.logical_or(jnp.logical_and(sel, past), blk == t), 0.0, NEG)
    kms[pl.ds(t, 1), :] = jnp.mean(k, axis=0, keepdims=True)

    vt = lax.dot_general(wvt_ref[...], h, NT, preferred_element_type=F32)
    vt_ref[0, 0] = vt.astype(BF16)

    sg_ref[0] = _silu(jnp.dot(h, wg_ref[...], preferred_element_type=F32))
    dx_ref[0] = jnp.dot(h, wd_ref[...], preferred_element_type=F32)
    sz_ref[0] = _silu(jnp.dot(h, wz_ref[...], preferred_element_type=F32))

    s = jnp.dot(h, ws_ref[...], preferred_element_type=F32)
    lane = lax.broadcasted_iota(jnp.int32, s.shape, 1)
    beta = jax.nn.sigmoid(s)
    z = s + dtb_ref[...]
    softplus = jnp.maximum(z, 0.0) + jnp.log1p(jnp.exp(-jnp.abs(z)))
    g = -jnp.exp(alog_ref[...]) * softplus
    gb = jnp.where(lane < DH, beta, jnp.where(lane < 2 * DH, g, 0.0))
    gb_ref[0] = gb
    gbt = gb.T[0:8, :]
    for cidx in range(TM // CH):
        gbt_ref[0, cidx] = gbt[:, cidx * CH:(cidx + 1) * CH]


def _inproj(x, mod3, pre_w, w_in, a_log, dt_bias):
    B, S, _ = x.shape
    nb = S // BLK
    nt = S // TM
    wb = w_in.astype(BF16)
    wqk = wb[:, 0:2 * MW]
    wvt = wb[:, 2 * MW:3 * MW].T
    wg = wb[:, 3 * MW:4 * MW]
    wd = wb[:, 4 * MW:4 * MW + 3 * DW]
    wz = wb[:, 4 * MW + 3 * DW:4 * MW + 4 * DW]
    ws = jnp.pad(wb[:, 4 * MW + 4 * DW:], ((0, 0), (0, LANES - 2 * DH)))
    alog = jnp.pad(a_log.astype(F32), (DH, LANES - 2 * DH)).reshape(1, LANES)
    dtb = jnp.pad(dt_bias.astype(F32), (DH, LANES - 2 * DH)).reshape(1, LANES)

    half = ROPE_D // 2
    inv_freq = jnp.power(ROPE_THETA, -jnp.arange(0, ROPE_D, 2, dtype=F32) / ROPE_D)
    ang = jnp.arange(S, dtype=jnp.int32).astype(F32)[:, None] * inv_freq[None, :]
    cos, sin = jnp.cos(ang), jnp.sin(ang)
    one = jnp.ones((S, MD - ROPE_D), F32)
    zero = jnp.zeros((S, MD - ROPE_D), F32)
    zh = jnp.zeros((S, half), F32)
    rc = jnp.tile(jnp.concatenate([cos, cos, one], axis=1), (1, 2))
    rs1 = jnp.tile(jnp.concatenate([-sin, zh, zero], axis=1), (1, 2))
    rs2 = jnp.tile(jnp.concatenate([zh, sin, zero], axis=1), (1, 2))

    const = lambda shape: pl.BlockSpec(shape, lambda b, t: (0,) * len(shape))
    row = lambda w: pl.BlockSpec((1, TM, w), lambda b, t: (b, t, 0))
    tab = pl.BlockSpec((TM, LANES), lambda b, t: (t, 0))
    out_shapes = [
        jax.ShapeDtypeStruct((B, S, MW), BF16),
        jax.ShapeDtypeStruct((B, S, MW), BF16),
        jax.ShapeDtypeStruct((B, MH, nb, S), F32),
        jax.ShapeDtypeStruct((B, nb, MW, BLK), BF16),
        jax.ShapeDtypeStruct((B, S, MW), F32),
        jax.ShapeDtypeStruct((B, S, 3 * DW), F32),
        jax.ShapeDtypeStruct((B, S, DW), F32),
        jax.ShapeDtypeStruct((B, S, LANES), F32),
        jax.ShapeDtypeStruct((B, S // CH, 8, CH), F32),
    ]
    out_specs = [
        row(MW), row(MW),
        pl.BlockSpec((1, MH, nb, TM), lambda b, t: (b, 0, 0, t)),
        pl.BlockSpec((1, 1, MW, BLK), lambda b, t: (b, t, 0, 0)),
        row(MW), row(3 * DW), row(DW), row(LANES),
        pl.BlockSpec((1, TM // CH, 8, CH), lambda b, t: (b, t, 0, 0)),
    ]
    return pl.pallas_call(
        functools.partial(_inproj_kernel, nb=nb),
        grid=(B, nt),
        in_specs=[row(D), pl.BlockSpec((1, 3, D), lambda b, t: (b, 0, 0)), const((1, D)),
                  const((D, 2 * MW)), const((MW, D)), const((D, MW)), const((D, 3 * DW)),
                  const((D, DW)), const((D, LANES)), tab, tab, tab,
                  const((1, LANES)), const((1, LANES))],
        out_specs=out_specs,
        out_shape=out_shapes,
        scratch_shapes=[pltpu.VMEM((nb, MW), F32)],
        compiler_params=pltpu.CompilerParams(
            dimension_semantics=("parallel", "arbitrary"), vmem_limit_bytes=VMEM_LIMIT),
        name="inproj",
    )(x, mod3, pre_w.reshape(1, D), wqk, wvt, wg, wd, wz, ws, rc, rs1, rs2, alog, dtb)


NPAIR = MW // LANES
ONES = 16


def _moba_kernel(q_ref, k_ref, vt_ref, bias_ref, sg_ref, o_ref,
                 pm_ref, s0_ref, s1_ref, qm_ref, acc_ref):
    i = pl.program_id(1)
    lane = lax.broadcasted_iota(jnp.int32, (1, LANES), 1)
    head_lanes = [lane < MD, lane >= MD]

    def fold8(a):
        return a.reshape(BLK // 8, 8, BLK)

    @pl.when(i == 0)
    def _():
        kpos = lax.broadcasted_iota(jnp.int32, (BLK, BLK), 0)
        qpos = lax.broadcasted_iota(jnp.int32, (BLK, BLK), 1)
        pm_ref[0] = jnp.zeros((BLK, BLK), F32)
        pm_ref[1] = jnp.where(kpos <= qpos, 0.0, NEG)
        pm_ref[2] = jnp.full((BLK, BLK), NEG, F32)

    ngroups = i // PVG + 1

    s_refs = (s0_ref, s1_ref)

    def scores(pr, qs, j, mx):
        kj = k_ref[0, pl.ds(pl.multiple_of(j * BLK, BLK), BLK), pr * LANES:(pr + 1) * LANES]
        pm = pm_ref[jnp.clip(j - i, -1, 1) + 1]
        out = []
        for h in range(2):
            s = lax.dot_general(kj, qs[h][...], NT, preferred_element_type=F32) + pm
            s_refs[pr % 2][h, j] = s
            out.append(jnp.maximum(mx[h], jnp.max(fold8(s), axis=0) + bias_ref[0, 2 * pr + h, pl.ds(j, 1), :]))
        return tuple(out)

    ones_rows = jnp.ones((ONES, BLK), BF16)

    def probs_pv(pr, m, j):
        for h in range(2):
            p = jnp.exp2(s_refs[pr % 2][h, j] - (m[h] - bias_ref[0, 2 * pr + h, pl.ds(j, 1), :]))
            vth = vt_ref[0, j, pr * LANES + h * MD:pr * LANES + (h + 1) * MD, :]
            acc_ref[h] += jnp.dot(jnp.concatenate([vth, ones_rows], axis=0), p.astype(BF16),
                                  preferred_element_type=F32)

    m = {}
    for stage in range(NPAIR + 1):
        p1 = stage if stage < NPAIR else None
        p2 = stage - 1 if stage >= 1 else None
        qs = None
        if p1 is not None:
            qp = q_ref[0, :, p1 * LANES:(p1 + 1) * LANES]
            for h in range(2):
                qm_ref[p1 % 2, h] = jnp.where(head_lanes[h], qp, jnp.zeros_like(qp))
            qs = [qm_ref.at[p1 % 2, h] for h in range(2)]
        if p2 is not None:
            acc_ref[...] = jnp.zeros_like(acc_ref)

        def body(t, mx):
            for dj in range(PVG):
                if p2 is not None:
                    probs_pv(p2, m[p2], t * PVG + dj)
                if p1 is not None:
                    mx = scores(p1, qs, t * PVG + dj, mx)
            return mx

        init = tuple(jnp.full((8, BLK), NEG, F32) for _ in range(2)) if p1 is not None else ()
        mx = lax.fori_loop(0, ngroups, body, init)
        if p1 is not None:
            m[p1] = [jnp.max(mx[h], axis=0, keepdims=True) for h in range(2)]
        if p2 is not None:
            ot = jnp.concatenate([acc_ref[h, 0:MD, :] / acc_ref[h, MD:MD + 1, :] for h in range(2)],
                                 axis=0)
            sl = slice(p2 * LANES, (p2 + 1) * LANES)
            o_ref[0, :, sl] = (ot.T * sg_ref[0, :, sl]).astype(o_ref.dtype)


def _moba(q, k, vt, bias, sg):
    B, S, _ = q.shape
    nb = S // BLK
    return pl.pallas_call(
        _moba_kernel,
        grid=(B, nb),
        in_specs=[pl.BlockSpec((1, BLK, MW), lambda b, i: (b, i, 0)),
                  pl.BlockSpec((1, S, MW), lambda b, i: (b, 0, 0)),
                  pl.BlockSpec((1, nb, MW, BLK), lambda b, i: (b, 0, 0, 0)),
                  pl.BlockSpec((1, MH, nb, BLK), lambda b, i: (b, 0, 0, i)),
                  pl.BlockSpec((1, BLK, MW), lambda b, i: (b, i, 0))],
        out_specs=pl.BlockSpec((1, BLK, MW), lambda b, i: (b, i, 0)),
        out_shape=jax.ShapeDtypeStruct((B, S, MW), BF16),
        scratch_shapes=[pltpu.VMEM((3, BLK, BLK), F32),
                        pltpu.VMEM((2, nb, BLK, BLK), F32),
                        pltpu.VMEM((2, nb, BLK, BLK), F32),
                        pltpu.VMEM((2, 2, BLK, LANES), BF16),
                        pltpu.VMEM((2, MD + ONES, BLK), F32)],
        compiler_params=pltpu.CompilerParams(
            dimension_semantics=("parallel", "arbitrary"), vmem_limit_bytes=VMEM_LIMIT),
        name="moba",
    )(q, k, vt, bias, sg)


def _dn_kernel(x_ref, sz_ref, gb_ref, gbt_ref, cw_ref, nw_ref, o_ref, xbuf, qkv, state, mqs, cs, os_, gls):
    t = pl.program_id(1)
    halo = 8

    @pl.when(t == 0)
    def _():
        xbuf[0:halo, :] = jnp.zeros((halo, 3 * DW), F32)
        state[...] = jnp.zeros_like(state)

    @pl.when(t > 0)
    def _():
        xbuf[0:halo, :] = xbuf[TD:TD + halo, :]

    xbuf[halo:halo + TD, :] = x_ref[0]

    for n in range(3 * DH):
        sl = slice(n * DK, (n + 1) * DK)
        y = jnp.zeros((TD, DK), F32)
        for tap in range(CONV):
            lo = halo - (CONV - 1) + tap
            y = y + cw_ref[tap:tap + 1, sl] * xbuf[lo:lo + TD, sl]
        y = _silu(y)
        if n < 2 * DH:
            y = y * lax.rsqrt(jnp.sum(y * y, axis=-1, keepdims=True) + EPS)
        if n < DH:
            y = y * (DK ** -0.5)
        qkv[:, sl] = y

    r = lax.broadcasted_iota(jnp.int32, (CH, CH), 0)
    c = lax.broadcasted_iota(jnp.int32, (CH, CH), 1)
    tril = r >= c
    r2 = lax.broadcasted_iota(jnp.int32, (CH, 2 * CH), 0)
    c2 = lax.broadcasted_iota(jnp.int32, (CH, 2 * CH), 1)
    c4 = lax.broadcasted_iota(jnp.int32, (CH, 2 * DK), 1)
    keep = jnp.logical_or(jnp.logical_and(c2 < CH, c2 >= r2), c2 - CH > r2)
    ones_l3 = jnp.concatenate([tril.astype(F32)] * 3, axis=1).astype(BF16)
    ones_u3 = jnp.concatenate([(r <= c).astype(F32)] * 3, axis=0).astype(BF16)
    nw = nw_ref[...]

    def local(it, _):
        units = [(dc, h) for dc in range(CU) for h in range(DH)]
        ci = [it * CU + dc for dc in range(CU)]
        r0 = [pl.multiple_of(c_ * CH, CH) for c_ in ci]
        gcol = [gb_ref[0, pl.ds(r0[dc], CH), :] for dc in range(CU)]
        grow = [gbt_ref[0, ci[dc]] for dc in range(CU)]
        gc = [jnp.dot(ones_l3, jnp.concatenate(_split3_bf16(g), axis=0).astype(BF16),
                      preferred_element_type=F32) for g in gcol]
        gr = [jnp.dot(jnp.concatenate(_split3_bf16(g), axis=1).astype(BF16), ones_u3,
                      preferred_element_type=F32) for g in grow]
        q = [qkv[pl.ds(r0[dc], CH), h * DK:(h + 1) * DK] for dc, h in units]
        k = [qkv[pl.ds(r0[dc], CH), DW + h * DK:DW + (h + 1) * DK] for dc, h in units]
        v = [qkv[pl.ds(r0[dc], CH), 2 * DW + h * DK:2 * DW + (h + 1) * DK] for dc, h in units]
        beta = [gcol[dc][:, h:h + 1] for dc, h in units]
        gcum = [gc[dc][:, DH + h:DH + h + 1] for dc, h in units]
        glast = [g[CH - 1:CH, :] for g in gcum]
        us = range(len(units))
        eg = [jnp.exp(g) for g in gcum]
        kb = [k[u] * beta[u] for u in us]
        dT = []
        for u, (dc, h) in enumerate(units):
            g2 = jnp.concatenate([gr[dc][DH + h:DH + h + 1, :]] * 2, axis=1)
            dT.append(jnp.where(keep, jnp.exp(jnp.where(keep, g2 - gcum[u], 0.0)), 0.0))
        aqt = [lax.dot_general(k[u].astype(BF16), jnp.concatenate([q[u], kb[u]], axis=0).astype(BF16), NT,
                               preferred_element_type=F32) for u in us]
        w = [jnp.concatenate([k[u] * jnp.exp(glast[u] - gcum[u]), aqt[u] * dT[u]], axis=1) for u in us]
        for lvl in range(6):
            for u in us:
                wh, wlo = _split_bf16(w[u])
                ph = pltpu.roll(wh[:, DK:], CH, axis=1)
                lhs = jnp.concatenate([jnp.where(c2 < CH, ph, wlo[:, DK:]), ph[:, :CH]], axis=1)
                res = jnp.dot(lhs.astype(BF16), jnp.concatenate([wh, wh, wlo], axis=0).astype(BF16),
                              preferred_element_type=F32)
                w[u] = jnp.where(c4 < 3 * CH, w[u] - res if lvl == 0 else w[u] + res, res)
        x0 = [jnp.concatenate([v[u] * beta[u], kb[u] * eg[u]], axis=1).astype(BF16) for u in us]
        rr = [jnp.dot(w[u].T.astype(BF16), x0[u], preferred_element_type=F32) for u in us]
        for dc in range(CU):
            gls[ci[dc]] = jnp.exp(gc[dc][CH - 1:CH, :])
        for u, (dc, h) in enumerate(units):
            mqs[h, ci[dc], 0:DK, :] = rr[u][0:DK, DK:].astype(BF16)
            mqs[h, ci[dc], DK:DK + CH, :] = (q[u] * eg[u] - rr[u][DK:DK + CH, DK:]).astype(BF16)
            cs[h, ci[dc]] = rr[u][0:DK, :DK]
            os_[pl.ds(r0[dc], CH), h * DK:(h + 1) * DK] = rr[u][DK:DK + CH, :DK]
        return 0

    lax.fori_loop(0, TD // (CH * CU), local, 0)

    def scan(ci, _):
        r0 = pl.multiple_of(ci * CH, CH)
        gl_all = gls[ci]
        for h in range(DH):
            st = state[h]
            rr = jnp.dot(mqs[h, ci], st.astype(BF16), preferred_element_type=F32)
            state[h] = st * gl_all[:, DH + h:DH + h + 1] - rr[:DK] + cs[h, ci]
            o = rr[DK:] + os_[pl.ds(r0, CH), h * DK:(h + 1) * DK]
            o = o * lax.rsqrt(jnp.mean(o * o, axis=-1, keepdims=True) + EPS) * nw
            o = o * sz_ref[0, pl.ds(r0, CH), h * DK:(h + 1) * DK]
            o_ref[0, pl.ds(r0, CH), h * DK:(h + 1) * DK] = o.astype(o_ref.dtype)
        return 0

    lax.fori_loop(0, TD // CH, scan, 0)


def _deltanet(dx, sz, gb, gbt, conv_w, dn_norm_w):
    B, S, _ = dx.shape
    nt = S // TD
    return pl.pallas_call(
        _dn_kernel,
        grid=(B, nt),
        in_specs=[pl.BlockSpec((1, TD, 3 * DW), lambda b, t: (b, t, 0)),
                  pl.BlockSpec((1, TD, DW), lambda b, t: (b, t, 0)),
                  pl.BlockSpec((1, TD, LANES), lambda b, t: (b, t, 0)),
                  pl.BlockSpec((1, TD // CH, 8, CH), lambda b, t: (b, t, 0, 0)),
                  pl.BlockSpec((CONV, 3 * DW), lambda b, t: (0, 0)),
                  pl.BlockSpec((1, DK), lambda b, t: (0, 0))],
        out_specs=pl.BlockSpec((1, TD, DW), lambda b, t: (b, t, 0)),
        out_shape=jax.ShapeDtypeStruct((B, S, DW), BF16),
        scratch_shapes=[pltpu.VMEM((TD + 8, 3 * DW), F32),
                        pltpu.VMEM((TD, 3 * DW), F32),
                        pltpu.VMEM((DH, DK, DK), F32),
                        pltpu.VMEM((DH, TD // CH, DK + CH, DK), BF16),
                        pltpu.VMEM((DH, TD // CH, DK, DK), F32),
                        pltpu.VMEM((TD, DW), F32),
                        pltpu.VMEM((TD // CH, 1, LANES), F32)],
        compiler_params=pltpu.CompilerParams(
            dimension_semantics=("parallel", "arbitrary"), vmem_limit_bytes=VMEM_LIMIT),
        name="deltanet",
    )(dx, sz, gb, gbt, conv_w, dn_norm_w.reshape(1, DK))


def _outproj_kernel(om_ref, od_ref, x_ref, mod_ref, pw_ref, wt_ref, wb_ref, o_ref):
    y = jnp.dot(om_ref[0], wt_ref[...], preferred_element_type=F32)
    y = y + jnp.dot(od_ref[0], wb_ref[...], preferred_element_type=F32)
    yn = y * lax.rsqrt(jnp.mean(y * y, axis=-1, keepdims=True) + EPS) * pw_ref[...]
    o_ref[0] = x_ref[0] + mod_ref[0, 2:3, :] * yn


def _outproj(om, od, x, mod3, post_w, w_out):
    B, S, _ = x.shape
    wb = w_out.astype(BF16)
    row = lambda w: pl.BlockSpec((1, TO, w), lambda b, t: (b, t, 0))
    return pl.pallas_call(
        _outproj_kernel,
        grid=(B, S // TO),
        in_specs=[row(MW), row(DW), row(D),
                  pl.BlockSpec((1, 3, D), lambda b, t: (b, 0, 0)),
                  pl.BlockSpec((1, D), lambda b, t: (0, 0)),
                  pl.BlockSpec((MW, D), lambda b, t: (0, 0)),
                  pl.BlockSpec((DW, D), lambda b, t: (0, 0))],
        out_specs=row(D),
        out_shape=jax.ShapeDtypeStruct((B, S, D), F32),
        compiler_params=pltpu.CompilerParams(
            dimension_semantics=("parallel", "parallel"), vmem_limit_bytes=VMEM_LIMIT),
        name="outproj",
    )(om, od, x, mod3, post_w.reshape(1, D), wb[:MW], wb[MW:])


def kernel(x, c, ada_w, ada_b, pre_norm_w, post_norm_w, w_in, conv_w, a_log, dt_bias, dn_norm_w, w_out):
    B, S, _ = x.shape
    assert S % TD == 0 and S % TO == 0 and S % (BLK * PVG) == 0 and TM == BLK
    depth = ada_w.shape[0]
    for l in range(depth):
        mod3 = _mod(c, ada_w[l], ada_b[l]).reshape(B, 3, D)
        q, k, bias, vt, sg, dx, sz, gb, gbt = _inproj(x, mod3, pre_norm_w[l], w_in[l], a_log[l], dt_bias[l])
        om = _moba(q, k, vt, bias, sg)
        od = _deltanet(dx, sz, gb, gbt, conv_w[l], dn_norm_w[l])
        x = _outproj(om, od, x, mod3, post_norm_w[l], w_out[l])
    return x
```

```python
import functools
import math

import jax
import jax.numpy as jnp
from jax import lax
from jax.experimental import pallas as pl
from jax.experimental.pallas import tpu as pltpu

F32 = jnp.float32
BF16 = jnp.bfloat16
HI = lax.Precision.HIGHEST

D = 1024
MH, MD = 8, 64
MW = MH * MD
BLK = 256
TOPK = 3
ROPE_D = 16
ROPE_THETA = 500000.0
DH, DK = 4, 128
DW = DH * DK
CONV = 4
CH = 64
EPS = 1e-6
NEG = -1e30
LOWEST = -3.0e38
LOG2E = math.log2(math.e)
PVG = 4

LANES = 128
VMEM_LIMIT = 56 * 1024 * 1024

TM = 256
TD = 512
CU = 2
TO = 1024

NT = (((1,), (1,)), ((), ()))
TN = (((0,), (0,)), ((), ()))


def _silu(v):
    h = 0.5 * v
    return h + h * jnp.tanh(h)


def _split_bf16(p):
    hi = p.astype(BF16).astype(F32)
    lo = (p - hi).astype(BF16).astype(F32)
    return hi, lo


def _split3_bf16(p):
    a, r = _split_bf16(p)
    c = (p - a - r).astype(BF16).astype(F32)
    return a, r, c


def _mod_kernel(c_ref, w_ref, b_ref, o_ref):
    c = c_ref[...]
    o_ref[...] = jnp.dot(_silu(c), w_ref[...], precision=HI, preferred_element_type=F32) + b_ref[...]


def _mod(c, ada_w, ada_b):
    B = c.shape[0]
    tn = 512
    return pl.pallas_call(
        _mod_kernel,
        grid=(3 * D // tn,),
        in_specs=[pl.BlockSpec((B, D), lambda j: (0, 0)),
                  pl.BlockSpec((D, tn), lambda j: (0, j)),
                  pl.BlockSpec((1, tn), lambda j: (0, j))],
        out_specs=pl.BlockSpec((B, tn), lambda j: (0, j)),
        out_shape=jax.ShapeDtypeStruct((B, 3 * D), F32),
        name="mod",
    )(c, ada_w, ada_b.reshape(1, 3 * D))


def _inproj_kernel(x_ref, mod_ref, pw_ref, wqk_ref, wvt_ref, wg_ref, wd_ref, wz_ref, ws_ref,
                   rc_ref, rs1_ref, rs2_ref, alog_ref, dtb_ref,
                   q_ref, k_ref, bias_ref, vt_ref, sg_ref, dx_ref, sz_ref, gb_ref, gbt_ref, kms, *, nb):
    t = pl.program_id(1)
    x = x_ref[0]
    shift = mod_ref[0, 0:1, :]
    scale = mod_ref[0, 1:2, :]
    y = x * lax.rsqrt(jnp.mean(x * x, axis=-1, keepdims=True) + EPS) * pw_ref[...]
    h = (y * (1.0 + scale) + shift).astype(BF16)

    rc, rs1, rs2 = rc_ref[...], rs1_ref[...], rs2_ref[...]

    def rope(t):
        tm1 = pltpu.roll(t, MW - ROPE_D // 2, axis=1)
        tp1 = pltpu.roll(t, ROPE_D // 2, axis=1)
        parts = []
        for n in range(MW // LANES):
            sl = slice(n * LANES, (n + 1) * LANES)
            parts.append(t[:, sl] * rc + tm1[:, sl] * rs1 + tp1[:, sl] * rs2)
        return jnp.concatenate(parts, axis=1)

    qk = jnp.dot(h, wqk_ref[...], preferred_element_type=F32)
    q = rope(qk[:, :MW])
    k = rope(qk[:, MW:])
    k_ref[0] = k.astype(BF16)
    q_ref[0] = (q * (MD ** -0.5 * LOG2E)).astype(BF16)

    vt = lax.dot_general(wvt_ref[...], h, NT, preferred_element_type=F32)
    vt_ref[0, 0] = vt.astype(BF16)

    sg_ref[0] = _silu(jnp.dot(h, wg_ref[...], preferred_element_type=F32))

    @pl.when(t == 0)
    def _():
        kms[...] = jnp.zeros_like(kms)

    km = kms[...]
    wl = lax.broadcasted_iota(jnp.int32, (1, MW), 1)
    kmh = jnp.concatenate([jnp.where((wl >= hh * MD) & (wl < (hh + 1) * MD), km, 0.0) for hh in range(MH)], axis=0)
    kh, kl = _split_bf16(kmh)
    qh, ql = _split_bf16(q)
    gate = lax.dot_general(jnp.concatenate([kh, kl, kh], axis=1).astype(BF16),
                           jnp.concatenate([qh, qh, ql], axis=1).astype(BF16), NT,
                           preferred_element_type=F32)
    blk = lax.broadcasted_iota(jnp.int32, (MH, nb, TM), 1)
    past = blk < t
    g3 = jnp.where(past, gate.reshape(MH, nb, TM), NEG)
    sel = jnp.zeros((MH, nb, TM), jnp.bool_)
    for _ in range(min(TOPK, nb)):
        gmax = jnp.max(g3, axis=1, keepdims=True)
        idx = jnp.min(jnp.where(g3 == gmax, blk, nb), axis=1, keepdims=True)
        pick = blk == idx
        sel = jnp.logical_or(sel, pick)
        g3 = jnp.where(pick, LOWEST, g3)
    bias_ref[0] = jnp.where(jnp.logical_or(jnp.logical_and(sel, past), blk == t), 0.0, NEG)
    kms[pl.ds(t, 1), :] = jnp.mean(k, axis=0, keepdims=True)

    s = jnp.dot(h, ws_ref[...], preferred_element_type=F32)
    lane = lax.broadcasted_iota(jnp.int32, s.shape, 1)
    beta = jax.nn.sigmoid(s)
    z = s + dtb_ref[...]
    softplus = jnp.maximum(z, 0.0) + jnp.log1p(jnp.exp(-jnp.abs(z)))
    g = -jnp.exp(alog_ref[...]) * softplus
    gb = jnp.where(lane < DH, beta, jnp.where(lane < 2 * DH, g, 0.0))
    gb_ref[0] = gb
    gbt = gb.T[0:8, :]
    for cidx in range(TM // CH):
        gbt_ref[0, cidx] = gbt[:, cidx * CH:(cidx + 1) * CH]

    sz_ref[0] = _silu(jnp.dot(h, wz_ref[...], preferred_element_type=F32))
    dx_ref[0] = jnp.dot(h, wd_ref[...], preferred_element_type=F32)


def _inproj(x, mod3, pre_w, w_in, a_log, dt_bias):
    B, S, _ = x.shape
    nb = S // BLK
    nt = S // TM
    wb = w_in.astype(BF16)
    wqk = wb[:, 0:2 * MW]
    wvt = wb[:, 2 * MW:3 * MW].T
    wg = wb[:, 3 * MW:4 * MW]
    wd = wb[:, 4 * MW:4 * MW + 3 * DW]
    wz = wb[:, 4 * MW + 3 * DW:4 * MW + 4 * DW]
    ws = jnp.pad(wb[:, 4 * MW + 4 * DW:], ((0, 0), (0, LANES - 2 * DH)))
    alog = jnp.pad(a_log.astype(F32), (DH, LANES - 2 * DH)).reshape(1, LANES)
    dtb = jnp.pad(dt_bias.astype(F32), (DH, LANES - 2 * DH)).reshape(1, LANES)

    half = ROPE_D // 2
    inv_freq = jnp.power(ROPE_THETA, -jnp.arange(0, ROPE_D, 2, dtype=F32) / ROPE_D)
    ang = jnp.arange(S, dtype=jnp.int32).astype(F32)[:, None] * inv_freq[None, :]
    cos, sin = jnp.cos(ang), jnp.sin(ang)
    one = jnp.ones((S, MD - ROPE_D), F32)
    zero = jnp.zeros((S, MD - ROPE_D), F32)
    zh = jnp.zeros((S, half), F32)
    rc = jnp.tile(jnp.concatenate([cos, cos, one], axis=1), (1, 2))
    rs1 = jnp.tile(jnp.concatenate([-sin, zh, zero], axis=1), (1, 2))
    rs2 = jnp.tile(jnp.concatenate([zh, sin, zero], axis=1), (1, 2))

    const = lambda shape: pl.BlockSpec(shape, lambda b, t: (0,) * len(shape))
    row = lambda w: pl.BlockSpec((1, TM, w), lambda b, t: (b, t, 0))
    tab = pl.BlockSpec((TM, LANES), lambda b, t: (t, 0))
    out_shapes = [
        jax.ShapeDtypeStruct((B, S, MW), BF16),
        jax.ShapeDtypeStruct((B, S, MW), BF16),
        jax.ShapeDtypeStruct((B, MH, nb, S), F32),
        jax.ShapeDtypeStruct((B, nb, MW, BLK), BF16),
        jax.ShapeDtypeStruct((B, S, MW), F32),
        jax.ShapeDtypeStruct((B, S, 3 * DW), F32),
        jax.ShapeDtypeStruct((B, S, DW), F32),
        jax.ShapeDtypeStruct((B, S, LANES), F32),
        jax.ShapeDtypeStruct((B, S // CH, 8, CH), F32),
    ]
    out_specs = [
        row(MW), row(MW),
        pl.BlockSpec((1, MH, nb, TM), lambda b, t: (b, 0, 0, t)),
        pl.BlockSpec((1, 1, MW, BLK), lambda b, t: (b, t, 0, 0)),
        row(MW), row(3 * DW), row(DW), row(LANES),
        pl.BlockSpec((1, TM // CH, 8, CH), lambda b, t: (b, t, 0, 0)),
    ]
    return pl.pallas_call(
        functools.partial(_inproj_kernel, nb=nb),
        grid=(B, nt),
        in_specs=[row(D), pl.BlockSpec((1, 3, D), lambda b, t: (b, 0, 0)), const((1, D)),
                  const((D, 2 * MW)), const((MW, D)), const((D, MW)), const((D, 3 * DW)),
                  const((D, DW)), const((D, LANES)), tab, tab, tab,
                  const((1, LANES)), const((1, LANES))],
        out_specs=out_specs,
        out_shape=out_shapes,
        scratch_shapes=[pltpu.VMEM((nb, MW), F32)],
        compiler_params=pltpu.CompilerParams(
            dimension_semantics=("parallel", "arbitrary"), vmem_limit_bytes=VMEM_LIMIT),
        name="inproj",
    )(x, mod3, pre_w.reshape(1, D), wqk, wvt, wg, wd, wz, ws, rc, rs1, rs2, alog, dtb)


NPAIR = MW // LANES
ONES = 16


def _moba_kernel(q_ref, k_ref, vt_ref, bias_ref, sg_ref, o_ref,
                 pm_ref, s0_ref, s1_ref, qm_ref, acc_ref):
    i = pl.program_id(1)
    lane = lax.broadcasted_iota(jnp.int32, (1, LANES), 1)
    head_lanes = [lane < MD, lane >= MD]

    def fold8(a):
        return a.reshape(BLK // 8, 8, BLK)

    @pl.when(i == 0)
    def _():
        kpos = lax.broadcasted_iota(jnp.int32, (BLK, BLK), 0)
        qpos = lax.broadcasted_iota(jnp.int32, (BLK, BLK), 1)
        pm_ref[0] = jnp.zeros((BLK, BLK), F32)
        pm_ref[1] = jnp.where(kpos <= qpos, 0.0, NEG)
        pm_ref[2] = jnp.full((BLK, BLK), NEG, F32)

    ngroups = i // PVG + 1

    s_refs = (s0_ref, s1_ref)

    def scores(pr, qs, j, mx):
        kj = k_ref[0, pl.ds(pl.multiple_of(j * BLK, BLK), BLK), pr * LANES:(pr + 1) * LANES]
        pm = pm_ref[jnp.clip(j - i, -1, 1) + 1]
        out = []
        for h in range(2):
            s = lax.dot_general(kj, qs[h][...], NT, preferred_element_type=F32) + pm
            s_refs[pr % 2][h, j] = s
            out.append(jnp.maximum(mx[h], jnp.max(fold8(s), axis=0) + bias_ref[0, 2 * pr + h, pl.ds(j, 1), :]))
        return tuple(out)

    ones_rows = jnp.ones((ONES, BLK), BF16)

    def probs_pv(pr, m, j):
        for h in range(2):
            p = jnp.exp2(s_refs[pr % 2][h, j] - (m[h] - bias_ref[0, 2 * pr + h, pl.ds(j, 1), :]))
            vth = vt_ref[0, j, pr * LANES + h * MD:pr * LANES + (h + 1) * MD, :]
            acc_ref[h] += jnp.dot(jnp.concatenate([vth, ones_rows], axis=0), p.astype(BF16),
                                  preferred_element_type=F32)

    m = {}
    for stage in range(NPAIR + 1):
        p1 = stage if stage < NPAIR else None
        p2 = stage - 1 if stage >= 1 else None
        qs = None
        if p1 is not None:
            qp = q_ref[0, :, p1 * LANES:(p1 + 1) * LANES]
            for h in range(2):
                qm_ref[p1 % 2, h] = jnp.where(head_lanes[h], qp, jnp.zeros_like(qp))
            qs = [qm_ref.at[p1 % 2, h] for h in range(2)]
        if p2 is not None:
            acc_ref[...] = jnp.zeros_like(acc_ref)

        def body(t, mx):
            for dj in range(PVG):
                if p1 is not None:
                    mx = scores(p1, qs, t * PVG + dj, mx)
                if p2 is not None:
                    probs_pv(p2, m[p2], t * PVG + dj)
            return mx

        init = tuple(jnp.full((8, BLK), NEG, F32) for _ in range(2)) if p1 is not None else ()
        mx = lax.fori_loop(0, ngroups, body, init)
        if p1 is not None:
            m[p1] = [jnp.max(mx[h], axis=0, keepdims=True) for h in range(2)]
        if p2 is not None:
            ot = jnp.concatenate([acc_ref[h, 0:MD, :] / acc_ref[h, MD:MD + 1, :] for h in range(2)],
                                 axis=0)
            sl = slice(p2 * LANES, (p2 + 1) * LANES)
            o_ref[0, :, sl] = (ot.T * sg_ref[0, :, sl]).astype(o_ref.dtype)


def _moba(q, k, vt, bias, sg):
    B, S, _ = q.shape
    nb = S // BLK
    return pl.pallas_call(
        _moba_kernel,
        grid=(B, nb),
        in_specs=[pl.BlockSpec((1, BLK, MW), lambda b, i: (b, i, 0)),
                  pl.BlockSpec((1, S, MW), lambda b, i: (b, 0, 0)),
                  pl.BlockSpec((1, nb, MW, BLK), lambda b, i: (b, 0, 0, 0)),
                  pl.BlockSpec((1, MH, nb, BLK), lambda b, i: (b, 0, 0, i)),
                  pl.BlockSpec((1, BLK, MW), lambda b, i: (b, i, 0))],
        out_specs=pl.BlockSpec((1, BLK, MW), lambda b, i: (b, i, 0)),
        out_shape=jax.ShapeDtypeStruct((B, S, MW), BF16),
        scratch_shapes=[pltpu.VMEM((3, BLK, BLK), F32),
                        pltpu.VMEM((2, nb, BLK, BLK), F32),
                        pltpu.VMEM((2, nb, BLK, BLK), F32),
                        pltpu.VMEM((2, 2, BLK, LANES), BF16),
                        pltpu.VMEM((2, MD + ONES, BLK), F32)],
        compiler_params=pltpu.CompilerParams(
            dimension_semantics=("parallel", "arbitrary"), vmem_limit_bytes=VMEM_LIMIT),
        name="moba",
    )(q, k, vt, bias, sg)


def _dn_kernel(x_ref, sz_ref, gb_ref, gbt_ref, cw_ref, nw_ref, o_ref, xbuf, qkv, state, mqs, cs, os_, gls):
    t = pl.program_id(1)
    halo = 8

    @pl.when(t == 0)
    def _():
        xbuf[0:halo, :] = jnp.zeros((halo, 3 * DW), F32)
        state[...] = jnp.zeros_like(state)

    @pl.when(t > 0)
    def _():
        xbuf[0:halo, :] = xbuf[TD:TD + halo, :]

    xbuf[halo:halo + TD, :] = x_ref[0]

    for n in range(3 * DH):
        sl = slice(n * DK, (n + 1) * DK)
        y = jnp.zeros((TD, DK), F32)
        for tap in range(CONV):
            lo = halo - (CONV - 1) + tap
            y = y + cw_ref[tap:tap + 1, sl] * xbuf[lo:lo + TD, sl]
        y = _silu(y)
        if n < 2 * DH:
            y = y * lax.rsqrt(jnp.sum(y * y, axis=-1, keepdims=True) + EPS)
        if n < DH:
            y = y * (DK ** -0.5)
        qkv[:, sl] = y

    r = lax.broadcasted_iota(jnp.int32, (CH, CH), 0)
    c = lax.broadcasted_iota(jnp.int32, (CH, CH), 1)
    tril = r >= c
    r2 = lax.broadcasted_iota(jnp.int32, (CH, 2 * CH), 0)
    c2 = lax.broadcasted_iota(jnp.int32, (CH, 2 * CH), 1)
    c4 = lax.broadcasted_iota(jnp.int32, (CH, 2 * DK), 1)
    zrows = jnp.zeros((CH, 2 * DK), F32)
    keep = jnp.logical_or(jnp.logical_and(c2 < CH, c2 >= r2), c2 - CH > r2)
    ones_l3 = jnp.concatenate([tril.astype(F32)] * 3, axis=1).astype(BF16)
    ones_u3 = jnp.concatenate([(r <= c).astype(F32)] * 3, axis=0).astype(BF16)
    nw = nw_ref[...]

    def local(it, _):
        units = [(dc, h) for dc in range(CU) for h in range(DH)]
        ci = [it * CU + dc for dc in range(CU)]
        r0 = [pl.multiple_of(c_ * CH, CH) for c_ in ci]
        gcol = [gb_ref[0, pl.ds(r0[dc], CH), :] for dc in range(CU)]
        grow = [gbt_ref[0, ci[dc]] for dc in range(CU)]
        gc = [jnp.dot(ones_l3, jnp.concatenate(_split3_bf16(g), axis=0).astype(BF16),
                      preferred_element_type=F32) for g in gcol]
        gr = [jnp.dot(jnp.concatenate(_split3_bf16(g), axis=1).astype(BF16), ones_u3,
                      preferred_element_type=F32) for g in grow]
        q = [qkv[pl.ds(r0[dc], CH), h * DK:(h + 1) * DK] for dc, h in units]
        k = [qkv[pl.ds(r0[dc], CH), DW + h * DK:DW + (h + 1) * DK] for dc, h in units]
        v = [qkv[pl.ds(r0[dc], CH), 2 * DW + h * DK:2 * DW + (h + 1) * DK] for dc, h in units]
        beta = [gcol[dc][:, h:h + 1] for dc, h in units]
        gcum = [gc[dc][:, DH + h:DH + h + 1] for dc, h in units]
        glast = [g[CH - 1:CH, :] for g in gcum]
        us = range(len(units))
        eg = [jnp.exp(g) for g in gcum]
        kb = [k[u] * beta[u] for u in us]
        dT = []
        for u, (dc, h) in enumerate(units):
            g2 = jnp.concatenate([gr[dc][DH + h:DH + h + 1, :]] * 2, axis=1)
            dT.append(jnp.where(keep, jnp.exp(jnp.where(keep, g2 - gcum[u], 0.0)), 0.0))
        aqt = [lax.dot_general(k[u].astype(BF16), jnp.concatenate([q[u], kb[u]], axis=0).astype(BF16), NT,
                               preferred_element_type=F32) for u in us]
        w = [jnp.concatenate([k[u] * jnp.exp(glast[u] - gcum[u]), aqt[u] * dT[u]], axis=1) for u in us]
        for lvl in range(6):
            for u in us:
                wh, wlo = _split_bf16(w[u])
                lhs = jnp.concatenate([wh[:, DK:], wlo[:, DK:], wh[:, DK:]], axis=1)
                rhs = jnp.concatenate([zrows, wh, zrows, wh, zrows, wlo], axis=0)
                res = jnp.dot(lhs.astype(BF16), rhs.astype(BF16), preferred_element_type=F32)
                w[u] = jnp.where(c4 < 3 * CH, w[u] - res if lvl == 0 else w[u] + res, res)
        x0 = [jnp.concatenate([v[u] * beta[u], kb[u] * eg[u]], axis=1).astype(BF16) for u in us]
        rr = [jnp.dot(w[u].T.astype(BF16), x0[u], preferred_element_type=F32) for u in us]
        for dc in range(CU):
            gls[ci[dc]] = jnp.exp(gc[dc][CH - 1:CH, :])
        for u, (dc, h) in enumerate(units):
            mqs[h, ci[dc], 0:DK, :] = rr[u][0:DK, DK:].astype(BF16)
            mqs[h, ci[dc], DK:DK + CH, :] = (q[u] * eg[u] - rr[u][DK:DK + CH, DK:]).astype(BF16)
            cs[h, ci[dc]] = rr[u][0:DK, :DK]
            os_[pl.ds(r0[dc], CH), h * DK:(h + 1) * DK] = rr[u][DK:DK + CH, :DK]
        return 0

    lax.fori_loop(0, TD // (CH * CU), local, 0)

    for ci in range(TD // CH):
        r0 = ci * CH
        gl_all = gls[ci]
        for h in range(DH):
            st = state[h]
            rr = jnp.dot(mqs[h, ci], st.astype(BF16), preferred_element_type=F32)
            state[h] = st * gl_all[:, DH + h:DH + h + 1] - rr[:DK] + cs[h, ci]
            o = rr[DK:] + os_[r0:r0 + CH, h * DK:(h + 1) * DK]
            o = o * lax.rsqrt(jnp.mean(o * o, axis=-1, keepdims=True) + EPS) * nw
            o = o * sz_ref[0, r0:r0 + CH, h * DK:(h + 1) * DK]
            o_ref[0, r0:r0 + CH, h * DK:(h + 1) * DK] = o.astype(o_ref.dtype)


def _deltanet(dx, sz, gb, gbt, conv_w, dn_norm_w):
    B, S, _ = dx.shape
    nt = S // TD
    return pl.pallas_call(
        _dn_kernel,
        grid=(B, nt),
        in_specs=[pl.BlockSpec((1, TD, 3 * DW), lambda b, t: (b, t, 0)),
                  pl.BlockSpec((1, TD, DW), lambda b, t: (b, t, 0)),
                  pl.BlockSpec((1, TD, LANES), lambda b, t: (b, t, 0)),
                  pl.BlockSpec((1, TD // CH, 8, CH), lambda b, t: (b, t, 0, 0)),
                  pl.BlockSpec((CONV, 3 * DW), lambda b, t: (0, 0)),
                  pl.BlockSpec((1, DK), lambda b, t: (0, 0))],
        out_specs=pl.BlockSpec((1, TD, DW), lambda b, t: (b, t, 0)),
        out_shape=jax.ShapeDtypeStruct((B, S, DW), BF16),
        scratch_shapes=[pltpu.VMEM((TD + 8, 3 * DW), F32),
                        pltpu.VMEM((TD, 3 * DW), F32),
                        pltpu.VMEM((DH, DK, DK), F32),
                        pltpu.VMEM((DH, TD // CH, DK + CH, DK), BF16),
                        pltpu.VMEM((DH, TD // CH, DK, DK), F32),
                        pltpu.VMEM((TD, DW), F32),
                        pltpu.VMEM((TD // CH, 1, LANES), F32)],
        compiler_params=pltpu.CompilerParams(
            dimension_semantics=("parallel", "arbitrary"), vmem_limit_bytes=VMEM_LIMIT),
        name="deltanet",
    )(dx, sz, gb, gbt, conv_w, dn_norm_w.reshape(1, DK))


def _outproj_kernel(om_ref, od_ref, x_ref, mod_ref, pw_ref, wt_ref, wb_ref, o_ref):
    y = jnp.dot(om_ref[0], wt_ref[...], preferred_element_type=F32)
    y = y + jnp.dot(od_ref[0], wb_ref[...], preferred_element_type=F32)
    yn = y * lax.rsqrt(jnp.mean(y * y, axis=-1, keepdims=True) + EPS) * pw_ref[...]
    o_ref[0] = x_ref[0] + mod_ref[0, 2:3, :] * yn


def _outproj(om, od, x, mod3, post_w, w_out):
    B, S, _ = x.shape
    wb = w_out.astype(BF16)
    row = lambda w: pl.BlockSpec((1, TO, w), lambda b, t: (b, t, 0))
    return pl.pallas_call(
        _outproj_kernel,
        grid=(B, S // TO),
        in_specs=[row(MW), row(DW), row(D),
                  pl.BlockSpec((1, 3, D), lambda b, t: (b, 0, 0)),
                  pl.BlockSpec((1, D), lambda b, t: (0, 0)),
                  pl.BlockSpec((MW, D), lambda b, t: (0, 0)),
                  pl.BlockSpec((DW, D), lambda b, t: (0, 0))],
        out_specs=row(D),
        out_shape=jax.ShapeDtypeStruct((B, S, D), F32),
        compiler_params=pltpu.CompilerParams(
            dimension_semantics=("parallel", "parallel"), vmem_limit_bytes=VMEM_LIMIT),
        name="outproj",
    )(om, od, x, mod3, post_w.reshape(1, D), wb[:MW], wb[MW:])


def kernel(x, c, ada_w, ada_b, pre_norm_w, post_norm_w, w_in, conv_w, a_log, dt_bias, dn_norm_w, w_out):
    B, S, _ = x.shape
    assert S % TD == 0 and S % TO == 0 and S % (BLK * PVG) == 0 and TM == BLK
    depth = ada_w.shape[0]
    for l in range(depth):
        mod3 = _mod(c, ada_w[l], ada_b[l]).reshape(B, 3, D)
        q, k, bias, vt, sg, dx, sz, gb, gbt = _inproj(x, mod3, pre_norm_w[l], w_in[l], a_log[l], dt_bias[l])
        om = _moba(q, k, vt, bias, sg)
        od = _deltanet(dx, sz, gb, gbt, conv_w[l], dn_norm_w[l])
        x = _outproj(om, od, x, mod3, post_norm_w[l], w_out[l])
    return x
```

```python
import functools
import math

import jax
import jax.numpy as jnp
from jax import lax
from jax.experimental import pallas as pl
from jax.experimental.pallas import tpu as pltpu

F32 = jnp.float32
BF16 = jnp.bfloat16
HI = lax.Precision.HIGHEST

D = 1024
MH, MD = 8, 64
MW = MH * MD
BLK = 256
TOPK = 3
ROPE_D = 16
ROPE_THETA = 500000.0
DH, DK = 4, 128
DW = DH * DK
CONV = 4
CH = 64
EPS = 1e-6
NEG = -1e30
LOWEST = -3.0e38
LOG2E = math.log2(math.e)
PVG = 4

LANES = 128
VMEM_LIMIT = 56 * 1024 * 1024

TM = 256
TD = 512
CU = 2
TO = 1024

NT = (((1,), (1,)), ((), ()))
TN = (((0,), (0,)), ((), ()))


def _silu(v):
    h = 0.5 * v
    return h + h * jnp.tanh(h)


def _split_bf16(p):
    hi = p.astype(BF16).astype(F32)
    lo = (p - hi).astype(BF16).astype(F32)
    return hi, lo


def _split3_bf16(p):
    a, r = _split_bf16(p)
    c = (p - a - r).astype(BF16).astype(F32)
    return a, r, c


def _mod_kernel(c_ref, w_ref, b_ref, o_ref):
    c = c_ref[...]
    o_ref[...] = jnp.dot(_silu(c), w_ref[...], precision=HI, preferred_element_type=F32) + b_ref[...]


def _mod(c, ada_w, ada_b):
    B = c.shape[0]
    tn = 512
    return pl.pallas_call(
        _mod_kernel,
        grid=(3 * D // tn,),
        in_specs=[pl.BlockSpec((B, D), lambda j: (0, 0)),
                  pl.BlockSpec((D, tn), lambda j: (0, j)),
                  pl.BlockSpec((1, tn), lambda j: (0, j))],
        out_specs=pl.BlockSpec((B, tn), lambda j: (0, j)),
        out_shape=jax.ShapeDtypeStruct((B, 3 * D), F32),
        name="mod",
    )(c, ada_w, ada_b.reshape(1, 3 * D))


def _inproj_kernel(x_ref, mod_ref, pw_ref, wqk_ref, wvt_ref, wg_ref, wd_ref, wz_ref, ws_ref,
                   rc_ref, rs1_ref, rs2_ref, alog_ref, dtb_ref,
                   q_ref, k_ref, bias_ref, vt_ref, sg_ref, dx_ref, sz_ref, gb_ref, gbt_ref, kms, *, nb):
    t = pl.program_id(1)
    x = x_ref[0]
    shift = mod_ref[0, 0:1, :]
    scale = mod_ref[0, 1:2, :]
    y = x * lax.rsqrt(jnp.mean(x * x, axis=-1, keepdims=True) + EPS) * pw_ref[...]
    h = (y * (1.0 + scale) + shift).astype(BF16)

    rc, rs1, rs2 = rc_ref[...], rs1_ref[...], rs2_ref[...]

    def rope(t):
        tm1 = pltpu.roll(t, MW - ROPE_D // 2, axis=1)
        tp1 = pltpu.roll(t, ROPE_D // 2, axis=1)
        parts = []
        for n in range(MW // LANES):
            sl = slice(n * LANES, (n + 1) * LANES)
            parts.append(t[:, sl] * rc + tm1[:, sl] * rs1 + tp1[:, sl] * rs2)
        return jnp.concatenate(parts, axis=1)

    qk = jnp.dot(h, wqk_ref[...], preferred_element_type=F32)
    q = rope(qk[:, :MW])
    k = rope(qk[:, MW:])
    k_ref[0] = k.astype(BF16)
    q_ref[0] = (q * (MD ** -0.5 * LOG2E)).astype(BF16)

    vt = lax.dot_general(wvt_ref[...], h, NT, preferred_element_type=F32)
    vt_ref[0, 0] = vt.astype(BF16)

    sg_ref[0] = _silu(jnp.dot(h, wg_ref[...], preferred_element_type=F32))

    @pl.when(t == 0)
    def _():
        kms[...] = jnp.zeros_like(kms)

    km = kms[...]
    wl = lax.broadcasted_iota(jnp.int32, (1, MW), 1)
    kmh = jnp.concatenate([jnp.where((wl >= hh * MD) & (wl < (hh + 1) * MD), km, 0.0) for hh in range(MH)], axis=0)
    kh, kl = _split_bf16(kmh)
    qh, ql = _split_bf16(q)
    gate = lax.dot_general(jnp.concatenate([kh, kl, kh], axis=1).astype(BF16),
                           jnp.concatenate([qh, qh, ql], axis=1).astype(BF16), NT,
                           preferred_element_type=F32)
    blk = lax.broadcasted_iota(jnp.int32, (MH, nb, TM), 1)
    past = blk < t
    g3 = jnp.where(past, gate.reshape(MH, nb, TM), NEG)
    sel = jnp.zeros((MH, nb, TM), jnp.bool_)
    for _ in range(min(TOPK, nb)):
        gmax = jnp.max(g3, axis=1, keepdims=True)
        idx = jnp.min(jnp.where(g3 == gmax, blk, nb), axis=1, keepdims=True)
        pick = blk == idx
        sel = jnp.logical_or(sel, pick)
        g3 = jnp.where(pick, LOWEST, g3)
    bias_ref[0] = jnp.where(jnp.logical_or(jnp.logical_and(sel, past), blk == t), 0.0, NEG)
    kms[pl.ds(t, 1), :] = jnp.mean(k, axis=0, keepdims=True)

    s = jnp.dot(h, ws_ref[...], preferred_element_type=F32)
    lane = lax.broadcasted_iota(jnp.int32, s.shape, 1)
    beta = jax.nn.sigmoid(s)
    z = s + dtb_ref[...]
    softplus = jnp.maximum(z, 0.0) + jnp.log1p(jnp.exp(-jnp.abs(z)))
    g = -jnp.exp(alog_ref[...]) * softplus
    gb = jnp.where(lane < DH, beta, jnp.where(lane < 2 * DH, g, 0.0))
    gb_ref[0] = gb
    gbt = gb.T[0:8, :]
    for cidx in range(TM // CH):
        gbt_ref[0, cidx] = gbt[:, cidx * CH:(cidx + 1) * CH]

    sz_ref[0] = _silu(jnp.dot(h, wz_ref[...], preferred_element_type=F32))
    dx_ref[0] = jnp.dot(h, wd_ref[...], preferred_element_type=F32)


def _inproj(x, mod3, pre_w, w_in, a_log, dt_bias):
    B, S, _ = x.shape
    nb = S // BLK
    nt = S // TM
    wb = w_in.astype(BF16)
    wqk = wb[:, 0:2 * MW]
    wvt = wb[:, 2 * MW:3 * MW].T
    wg = wb[:, 3 * MW:4 * MW]
    wd = wb[:, 4 * MW:4 * MW + 3 * DW]
    wz = wb[:, 4 * MW + 3 * DW:4 * MW + 4 * DW]
    ws = jnp.pad(wb[:, 4 * MW + 4 * DW:], ((0, 0), (0, LANES - 2 * DH)))
    alog = jnp.pad(a_log.astype(F32), (DH, LANES - 2 * DH)).reshape(1, LANES)
    dtb = jnp.pad(dt_bias.astype(F32), (DH, LANES - 2 * DH)).reshape(1, LANES)

    half = ROPE_D // 2
    inv_freq = jnp.power(ROPE_THETA, -jnp.arange(0, ROPE_D, 2, dtype=F32) / ROPE_D)
    ang = jnp.arange(S, dtype=jnp.int32).astype(F32)[:, None] * inv_freq[None, :]
    cos, sin = jnp.cos(ang), jnp.sin(ang)
    one = jnp.ones((S, MD - ROPE_D), F32)
    zero = jnp.zeros((S, MD - ROPE_D), F32)
    zh = jnp.zeros((S, half), F32)
    rc = jnp.tile(jnp.concatenate([cos, cos, one], axis=1), (1, 2))
    rs1 = jnp.tile(jnp.concatenate([-sin, zh, zero], axis=1), (1, 2))
    rs2 = jnp.tile(jnp.concatenate([zh, sin, zero], axis=1), (1, 2))

    const = lambda shape: pl.BlockSpec(shape, lambda b, t: (0,) * len(shape))
    row = lambda w: pl.BlockSpec((1, TM, w), lambda b, t: (b, t, 0))
    tab = pl.BlockSpec((TM, LANES), lambda b, t: (t, 0))
    out_shapes = [
        jax.ShapeDtypeStruct((B, S, MW), BF16),
        jax.ShapeDtypeStruct((B, S, MW), BF16),
        jax.ShapeDtypeStruct((B, MH, nb, S), F32),
        jax.ShapeDtypeStruct((B, nb, MW, BLK), BF16),
        jax.ShapeDtypeStruct((B, S, MW), F32),
        jax.ShapeDtypeStruct((B, S, 3 * DW), F32),
        jax.ShapeDtypeStruct((B, S, DW), F32),
        jax.ShapeDtypeStruct((B, S, LANES), F32),
        jax.ShapeDtypeStruct((B, S // CH, 8, CH), F32),
    ]
    out_specs = [
        row(MW), row(MW),
        pl.BlockSpec((1, MH, nb, TM), lambda b, t: (b, 0, 0, t)),
        pl.BlockSpec((1, 1, MW, BLK), lambda b, t: (b, t, 0, 0)),
        row(MW), row(3 * DW), row(DW), row(LANES),
        pl.BlockSpec((1, TM // CH, 8, CH), lambda b, t: (b, t, 0, 0)),
    ]
    return pl.pallas_call(
        functools.partial(_inproj_kernel, nb=nb),
        grid=(B, nt),
        in_specs=[row(D), pl.BlockSpec((1, 3, D), lambda b, t: (b, 0, 0)), const((1, D)),
                  const((D, 2 * MW)), const((MW, D)), const((D, MW)), const((D, 3 * DW)),
                  const((D, DW)), const((D, LANES)), tab, tab, tab,
                  const((1, LANES)), const((1, LANES))],
        out_specs=out_specs,
        out_shape=out_shapes,
        scratch_shapes=[pltpu.VMEM((nb, MW), F32)],
        compiler_params=pltpu.CompilerParams(
            dimension_semantics=("parallel", "arbitrary"), vmem_limit_bytes=VMEM_LIMIT),
        name="inproj",
    )(x, mod3, pre_w.reshape(1, D), wqk, wvt, wg, wd, wz, ws, rc, rs1, rs2, alog, dtb)


NPAIR = MW // LANES
ONES = 16


def _moba_kernel(q_ref, k_ref, vt_ref, bias_ref, sg_ref, o_ref,
                 pm_ref, s0_ref, s1_ref, qm_ref, acc_ref):
    i = pl.program_id(1)
    lane = lax.broadcasted_iota(jnp.int32, (1, LANES), 1)
    head_lanes = [lane < MD, lane >= MD]

    def fold8(a):
        return a.reshape(BLK // 8, 8, BLK)

    @pl.when(i == 0)
    def _():
        kpos = lax.broadcasted_iota(jnp.int32, (BLK, BLK), 0)
        qpos = lax.broadcasted_iota(jnp.int32, (BLK, BLK), 1)
        pm_ref[0] = jnp.zeros((BLK, BLK), F32)
        pm_ref[1] = jnp.where(kpos <= qpos, 0.0, NEG)
        pm_ref[2] = jnp.full((BLK, BLK), NEG, F32)

    ngroups = i // PVG + 1

    s_refs = (s0_ref, s1_ref)

    def scores(pr, qs, j, mx):
        kj = k_ref[0, pl.ds(pl.multiple_of(j * BLK, BLK), BLK), pr * LANES:(pr + 1) * LANES]
        pm = pm_ref[jnp.clip(j - i, -1, 1) + 1]
        out = []
        for h in range(2):
            s = lax.dot_general(kj, qs[h][...], NT, preferred_element_type=F32) + pm
            s_refs[pr % 2][h, j] = s
            out.append(jnp.maximum(mx[h], jnp.max(fold8(s), axis=0) + bias_ref[0, 2 * pr + h, pl.ds(j, 1), :]))
        return tuple(out)

    ones_rows = jnp.ones((ONES, BLK), BF16)

    def probs_pv(pr, m, j):
        for h in range(2):
            p = jnp.exp2(s_refs[pr % 2][h, j] - (m[h] - bias_ref[0, 2 * pr + h, pl.ds(j, 1), :]))
            vth = vt_ref[0, j, pr * LANES + h * MD:pr * LANES + (h + 1) * MD, :]
            acc_ref[h] += jnp.dot(jnp.concatenate([vth, ones_rows], axis=0), p.astype(BF16),
                                  preferred_element_type=F32)

    m = {}
    for stage in range(NPAIR + 1):
        p1 = stage if stage < NPAIR else None
        p2 = stage - 1 if stage >= 1 else None
        qs = None
        if p1 is not None:
            qp = q_ref[0, :, p1 * LANES:(p1 + 1) * LANES]
            for h in range(2):
                qm_ref[p1 % 2, h] = jnp.where(head_lanes[h], qp, jnp.zeros_like(qp))
            qs = [qm_ref.at[p1 % 2, h] for h in range(2)]
        if p2 is not None:
            acc_ref[...] = jnp.zeros_like(acc_ref)

        def body(t, mx):
            for dj in range(PVG):
                if p1 is not None:
                    mx = scores(p1, qs, t * PVG + dj, mx)
                if p2 is not None:
                    probs_pv(p2, m[p2], t * PVG + dj)
            return mx

        init = tuple(jnp.full((8, BLK), NEG, F32) for _ in range(2)) if p1 is not None else ()
        mx = lax.fori_loop(0, ngroups, body, init)
        if p1 is not None:
            m[p1] = [jnp.max(mx[h], axis=0, keepdims=True) for h in range(2)]
        if p2 is not None:
            ot = jnp.concatenate([acc_ref[h, 0:MD, :] / acc_ref[h, MD:MD + 1, :] for h in range(2)],
                                 axis=0)
            sl = slice(p2 * LANES, (p2 + 1) * LANES)
            o_ref[0, :, sl] = (ot.T * sg_ref[0, :, sl]).astype(o_ref.dtype)


def _moba(q, k, vt, bias, sg):
    B, S, _ = q.shape
    nb = S // BLK
    return pl.pallas_call(
        _moba_kernel,
        grid=(B, nb),
        in_specs=[pl.BlockSpec((1, BLK, MW), lambda b, i: (b, i, 0)),
                  pl.BlockSpec((1, S, MW), lambda b, i: (b, 0, 0)),
                  pl.BlockSpec((1, nb, MW, BLK), lambda b, i: (b, 0, 0, 0)),
                  pl.BlockSpec((1, MH, nb, BLK), lambda b, i: (b, 0, 0, i)),
                  pl.BlockSpec((1, BLK, MW), lambda b, i: (b, i, 0))],
        out_specs=pl.BlockSpec((1, BLK, MW), lambda b, i: (b, i, 0)),
        out_shape=jax.ShapeDtypeStruct((B, S, MW), BF16),
        scratch_shapes=[pltpu.VMEM((3, BLK, BLK), F32),
                        pltpu.VMEM((2, nb, BLK, BLK), F32),
                        pltpu.VMEM((2, nb, BLK, BLK), F32),
                        pltpu.VMEM((2, 2, BLK, LANES), BF16),
                        pltpu.VMEM((2, MD + ONES, BLK), F32)],
        compiler_params=pltpu.CompilerParams(
            dimension_semantics=("parallel", "arbitrary"), vmem_limit_bytes=VMEM_LIMIT),
        name="moba",
    )(q, k, vt, bias, sg)


def _dn_kernel(x_ref, sz_ref, gb_ref, gbt_ref, cw_ref, nw_ref, o_ref, xbuf, qkv, state, mqs, cs, os_, gls):
    t = pl.program_id(1)
    halo = 8

    @pl.when(t == 0)
    def _():
        xbuf[0:halo, :] = jnp.zeros((halo, 3 * DW), F32)
        state[...] = jnp.zeros_like(state)

    @pl.when(t > 0)
    def _():
        xbuf[0:halo, :] = xbuf[TD:TD + halo, :]

    xbuf[halo:halo + TD, :] = x_ref[0]

    for n in range(3 * DH):
        sl = slice(n * DK, (n + 1) * DK)
        y = jnp.zeros((TD, DK), F32)
        for tap in range(CONV):
            lo = halo - (CONV - 1) + tap
            y = y + cw_ref[tap:tap + 1, sl] * xbuf[lo:lo + TD, sl]
        y = _silu(y)
        if n < 2 * DH:
            y = y * lax.rsqrt(jnp.sum(y * y, axis=-1, keepdims=True) + EPS)
        if n < DH:
            y = y * (DK ** -0.5)
        qkv[:, sl] = y

    r = lax.broadcasted_iota(jnp.int32, (CH, CH), 0)
    c = lax.broadcasted_iota(jnp.int32, (CH, CH), 1)
    tril = r >= c
    r2 = lax.broadcasted_iota(jnp.int32, (CH, 2 * CH), 0)
    c2 = lax.broadcasted_iota(jnp.int32, (CH, 2 * CH), 1)
    c4 = lax.broadcasted_iota(jnp.int32, (CH, 2 * DK), 1)
    keep = jnp.logical_or(jnp.logical_and(c2 < CH, c2 >= r2), c2 - CH > r2)
    ones_l3 = jnp.concatenate([tril.astype(F32)] * 3, axis=1).astype(BF16)
    ones_u3 = jnp.concatenate([(r <= c).astype(F32)] * 3, axis=0).astype(BF16)
    nw = nw_ref[...]

    def local(it, _):
        units = [(dc, h) for dc in range(CU) for h in range(DH)]
        ci = [it * CU + dc for dc in range(CU)]
        r0 = [pl.multiple_of(c_ * CH, CH) for c_ in ci]
        gcol = [gb_ref[0, pl.ds(r0[dc], CH), :] for dc in range(CU)]
        grow = [gbt_ref[0, ci[dc]] for dc in range(CU)]
        gc = [jnp.dot(ones_l3, jnp.concatenate(_split3_bf16(g), axis=0).astype(BF16),
                      preferred_element_type=F32) for g in gcol]
        gr = [jnp.dot(jnp.concatenate(_split3_bf16(g), axis=1).astype(BF16), ones_u3,
                      preferred_element_type=F32) for g in grow]
        q = [qkv[pl.ds(r0[dc], CH), h * DK:(h + 1) * DK] for dc, h in units]
        k = [qkv[pl.ds(r0[dc], CH), DW + h * DK:DW + (h + 1) * DK] for dc, h in units]
        v = [qkv[pl.ds(r0[dc], CH), 2 * DW + h * DK:2 * DW + (h + 1) * DK] for dc, h in units]
        beta = [gcol[dc][:, h:h + 1] for dc, h in units]
        gcum = [gc[dc][:, DH + h:DH + h + 1] for dc, h in units]
        glast = [g[CH - 1:CH, :] for g in gcum]
        us = range(len(units))
        eg = [jnp.exp(g) for g in gcum]
        kb = [k[u] * beta[u] for u in us]
        dT = []
        for u, (dc, h) in enumerate(units):
            g2 = jnp.concatenate([gr[dc][DH + h:DH + h + 1, :]] * 2, axis=1)
            dT.append(jnp.where(keep, jnp.exp(jnp.where(keep, g2 - gcum[u], 0.0)), 0.0))
        aqt = [lax.dot_general(k[u].astype(BF16), jnp.concatenate([q[u], kb[u]], axis=0).astype(BF16), NT,
                               preferred_element_type=F32) for u in us]
        w = [jnp.concatenate([k[u] * jnp.exp(glast[u] - gcum[u]), aqt[u] * dT[u]], axis=1) for u in us]
        for lvl in range(6):
            for u in us:
                wh, wlo = _split_bf16(w[u])
                ph = pltpu.roll(wh[:, DK:], CH, axis=1)
                lhs = jnp.concatenate([jnp.where(c2 < CH, ph, wlo[:, DK:]), ph[:, :CH]], axis=1)
                res = jnp.dot(lhs.astype(BF16), jnp.concatenate([wh, wh, wlo], axis=0).astype(BF16),
                              preferred_element_type=F32)
                w[u] = jnp.where(c4 < 3 * CH, w[u] - res if lvl == 0 else w[u] + res, res)
        x0 = [jnp.concatenate([v[u] * beta[u], kb[u] * eg[u]], axis=1).astype(BF16) for u in us]
        rr = [jnp.dot(w[u].T.astype(BF16), x0[u], preferred_element_type=F32) for u in us]
        for dc in range(CU):
            gls[ci[dc]] = jnp.exp(gc[dc][CH - 1:CH, :])
        for u, (dc, h) in enumerate(units):
            mqs[h, ci[dc], 0:DK, :] = rr[u][0:DK, DK:].astype(BF16)
            mqs[h, ci[dc], DK:DK + CH, :] = (q[u] * eg[u] - rr[u][DK:DK + CH, DK:]).astype(BF16)
            cs[h, ci[dc]] = rr[u][0:DK, :DK]
            os_[pl.ds(r0[dc], CH), h * DK:(h + 1) * DK] = rr[u][DK:DK + CH, :DK]
        return 0

    lax.fori_loop(0, TD // (CH * CU), local, 0)

    for ci in range(TD // CH):
        r0 = ci * CH
        gl_all = gls[ci]
        for h in range(DH):
            st = state[h]
            rr = jnp.dot(mqs[h, ci], st.astype(BF16), preferred_element_type=F32)
            state[h] = st * gl_all[:, DH + h:DH + h + 1] - rr[:DK] + cs[h, ci]
            o = rr[DK:] + os_[r0:r0 + CH, h * DK:(h + 1) * DK]
            o = o * lax.rsqrt(jnp.mean(o * o, axis=-1, keepdims=True) + EPS) * nw
            o = o * sz_ref[0, r0:r0 + CH, h * DK:(h + 1) * DK]
            o_ref[0, r0:r0 + CH, h * DK:(h + 1) * DK] = o.astype(o_ref.dtype)


def _deltanet(dx, sz, gb, gbt, conv_w, dn_norm_w):
    B, S, _ = dx.shape
    nt = S // TD
    return pl.pallas_call(
        _dn_kernel,
        grid=(B, nt),
        in_specs=[pl.BlockSpec((1, TD, 3 * DW), lambda b, t: (b, t, 0)),
                  pl.BlockSpec((1, TD, DW), lambda b, t: (b, t, 0)),
                  pl.BlockSpec((1, TD, LANES), lambda b, t: (b, t, 0)),
                  pl.BlockSpec((1, TD // CH, 8, CH), lambda b, t: (b, t, 0, 0)),
                  pl.BlockSpec((CONV, 3 * DW), lambda b, t: (0, 0)),
                  pl.BlockSpec((1, DK), lambda b, t: (0, 0))],
        out_specs=pl.BlockSpec((1, TD, DW), lambda b, t: (b, t, 0)),
        out_shape=jax.ShapeDtypeStruct((B, S, DW), BF16),
        scratch_shapes=[pltpu.VMEM((TD + 8, 3 * DW), F32),
                        pltpu.VMEM((TD, 3 * DW), F32),
                        pltpu.VMEM((DH, DK, DK), F32),
                        pltpu.VMEM((DH, TD // CH, DK + CH, DK), BF16),
                        pltpu.VMEM((DH, TD // CH, DK, DK), F32),
                        pltpu.VMEM((TD, DW), F32),
                        pltpu.VMEM((TD // CH, 1, LANES), F32)],
        compiler_params=pltpu.CompilerParams(
            dimension_semantics=("parallel", "arbitrary"), vmem_limit_bytes=VMEM_LIMIT),
        name="deltanet",
    )(dx, sz, gb, gbt, conv_w, dn_norm_w.reshape(1, DK))


def _outproj_kernel(om_ref, od_ref, x_ref, mod_ref, pw_ref, wt_ref, wb_ref, o_ref):
    y = jnp.dot(om_ref[0], wt_ref[...], preferred_element_type=F32)
    y = y + jnp.dot(od_ref[0], wb_ref[...], preferred_element_type=F32)
    yn = y * lax.rsqrt(jnp.mean(y * y, axis=-1, keepdims=True) + EPS) * pw_ref[...]
    o_ref[0] = x_ref[0] + mod_ref[0, 2:3, :] * yn


def _outproj(om, od, x, mod3, post_w, w_out):
    B, S, _ = x.shape
    wb = w_out.astype(BF16)
    row = lambda w: pl.BlockSpec((1, TO, w), lambda b, t: (b, t, 0))
    return pl.pallas_call(
        _outproj_kernel,
        grid=(B, S // TO),
        in_specs=[row(MW), row(DW), row(D),
                  pl.BlockSpec((1, 3, D), lambda b, t: (b, 0, 0)),
                  pl.BlockSpec((1, D), lambda b, t: (0, 0)),
                  pl.BlockSpec((MW, D), lambda b, t: (0, 0)),
                  pl.BlockSpec((DW, D), lambda b, t: (0, 0))],
        out_specs=row(D),
        out_shape=jax.ShapeDtypeStruct((B, S, D), F32),
        compiler_params=pltpu.CompilerParams(
            dimension_semantics=("parallel", "parallel"), vmem_limit_bytes=VMEM_LIMIT),
        name="outproj",
    )(om, od, x, mod3, post_w.reshape(1, D), wb[:MW], wb[MW:])


def kernel(x, c, ada_w, ada_b, pre_norm_w, post_norm_w, w_in, conv_w, a_log, dt_bias, dn_norm_w, w_out):
    B, S, _ = x.shape
    assert S % TD == 0 and S % TO == 0 and S % (BLK * PVG) == 0 and TM == BLK
    depth = ada_w.shape[0]
    for l in range(depth):
        mod3 = _mod(c, ada_w[l], ada_b[l]).reshape(B, 3, D)
        q, k, bias, vt, sg, dx, sz, gb, gbt = _inproj(x, mod3, pre_norm_w[l], w_in[l], a_log[l], dt_bias[l])
        om = _moba(q, k, vt, bias, sg)
        od = _deltanet(dx, sz, gb, gbt, conv_w[l], dn_norm_w[l])
        x = _outproj(om, od, x, mod3, post_norm_w[l], w_out[l])
    return x
```

```python
import functools
import math

import jax
import jax.numpy as jnp
from jax import lax
from jax.experimental import pallas as pl
from jax.experimental.pallas import tpu as pltpu

F32 = jnp.float32
BF16 = jnp.bfloat16
HI = lax.Precision.HIGHEST

D = 1024
MH, MD = 8, 64
MW = MH * MD
BLK = 256
TOPK = 3
ROPE_D = 16
ROPE_THETA = 500000.0
DH, DK = 4, 128
DW = DH * DK
CONV = 4
CH = 64
EPS = 1e-6
NEG = -1e30
LOWEST = -3.0e38
LOG2E = math.log2(math.e)
PVG = 4

LANES = 128
VMEM_LIMIT = 56 * 1024 * 1024

TM = 256
TD = 512
CU = 4
TO = 1024

NT = (((1,), (1,)), ((), ()))
TN = (((0,), (0,)), ((), ()))


def _silu(v):
    h = 0.5 * v
    return h + h * jnp.tanh(h)


def _split_bf16(p):
    hi = p.astype(BF16).astype(F32)
    lo = (p - hi).astype(BF16).astype(F32)
    return hi, lo


def _split3_bf16(p):
    a, r = _split_bf16(p)
    c = (p - a - r).astype(BF16).astype(F32)
    return a, r, c


def _mod_kernel(c_ref, w_ref, b_ref, o_ref):
    c = c_ref[...]
    o_ref[...] = jnp.dot(_silu(c), w_ref[...], precision=HI, preferred_element_type=F32) + b_ref[...]


def _mod(c, ada_w, ada_b):
    B = c.shape[0]
    tn = 512
    return pl.pallas_call(
        _mod_kernel,
        grid=(3 * D // tn,),
        in_specs=[pl.BlockSpec((B, D), lambda j: (0, 0)),
                  pl.BlockSpec((D, tn), lambda j: (0, j)),
                  pl.BlockSpec((1, tn), lambda j: (0, j))],
        out_specs=pl.BlockSpec((B, tn), lambda j: (0, j)),
        out_shape=jax.ShapeDtypeStruct((B, 3 * D), F32),
        name="mod",
    )(c, ada_w, ada_b.reshape(1, 3 * D))


def _inproj_kernel(x_ref, mod_ref, pw_ref, wqk_ref, wvt_ref, wg_ref, wd_ref, wz_ref, ws_ref,
                   rc_ref, rs1_ref, rs2_ref, alog_ref, dtb_ref,
                   q_ref, k_ref, bias_ref, vt_ref, sg_ref, dx_ref, sz_ref, gb_ref, gbt_ref, kms, *, nb):
    t = pl.program_id(1)
    x = x_ref[0]
    shift = mod_ref[0, 0:1, :]
    scale = mod_ref[0, 1:2, :]
    y = x * lax.rsqrt(jnp.mean(x * x, axis=-1, keepdims=True) + EPS) * pw_ref[...]
    h = (y * (1.0 + scale) + shift).astype(BF16)

    rc, rs1, rs2 = rc_ref[...], rs1_ref[...], rs2_ref[...]

    def rope(t):
        tm1 = pltpu.roll(t, MW - ROPE_D // 2, axis=1)
        tp1 = pltpu.roll(t, ROPE_D // 2, axis=1)
        parts = []
        for n in range(MW // LANES):
            sl = slice(n * LANES, (n + 1) * LANES)
            parts.append(t[:, sl] * rc + tm1[:, sl] * rs1 + tp1[:, sl] * rs2)
        return jnp.concatenate(parts, axis=1)

    qk = jnp.dot(h, wqk_ref[...], preferred_element_type=F32)
    q = rope(qk[:, :MW])
    k = rope(qk[:, MW:])
    k_ref[0] = k.astype(BF16)
    q_ref[0] = (q * (MD ** -0.5 * LOG2E)).astype(BF16)

    vt = lax.dot_general(wvt_ref[...], h, NT, preferred_element_type=F32)
    vt_ref[0, 0] = vt.astype(BF16)

    sg_ref[0] = _silu(jnp.dot(h, wg_ref[...], preferred_element_type=F32))

    @pl.when(t == 0)
    def _():
        kms[...] = jnp.zeros_like(kms)

    km = kms[...]
    wl = lax.broadcasted_iota(jnp.int32, (1, MW), 1)
    kmh = jnp.concatenate([jnp.where((wl >= hh * MD) & (wl < (hh + 1) * MD), km, 0.0) for hh in range(MH)], axis=0)
    kh, kl = _split_bf16(kmh)
    qh, ql = _split_bf16(q)
    gate = lax.dot_general(jnp.concatenate([kh, kl, kh], axis=1).astype(BF16),
                           jnp.concatenate([qh, qh, ql], axis=1).astype(BF16), NT,
                           preferred_element_type=F32)
    blk = lax.broadcasted_iota(jnp.int32, (MH, nb, TM), 1)
    past = blk < t
    g3 = jnp.where(past, gate.reshape(MH, nb, TM), NEG)
    sel = jnp.zeros((MH, nb, TM), jnp.bool_)
    for _ in range(min(TOPK, nb)):
        gmax = jnp.max(g3, axis=1, keepdims=True)
        idx = jnp.min(jnp.where(g3 == gmax, blk, nb), axis=1, keepdims=True)
        pick = blk == idx
        sel = jnp.logical_or(sel, pick)
        g3 = jnp.where(pick, LOWEST, g3)
    bias_ref[0] = jnp.where(jnp.logical_or(jnp.logical_and(sel, past), blk == t), 0.0, NEG)
    kms[pl.ds(t, 1), :] = jnp.mean(k, axis=0, keepdims=True)

    s = jnp.dot(h, ws_ref[...], preferred_element_type=F32)
    lane = lax.broadcasted_iota(jnp.int32, s.shape, 1)
    beta = jax.nn.sigmoid(s)
    z = s + dtb_ref[...]
    softplus = jnp.maximum(z, 0.0) + jnp.log1p(jnp.exp(-jnp.abs(z)))
    g = -jnp.exp(alog_ref[...]) * softplus
    gb = jnp.where(lane < DH, beta, jnp.where(lane < 2 * DH, g, 0.0))
    gb_ref[0] = gb
    gbt = gb.T[0:8, :]
    for cidx in range(TM // CH):
        gbt_ref[0, cidx] = gbt[:, cidx * CH:(cidx + 1) * CH]

    sz_ref[0] = _silu(jnp.dot(h, wz_ref[...], preferred_element_type=F32))
    dx_ref[0] = jnp.dot(h, wd_ref[...], preferred_element_type=F32)


def _inproj(x, mod3, pre_w, w_in, a_log, dt_bias):
    B, S, _ = x.shape
    nb = S // BLK
    nt = S // TM
    wb = w_in.astype(BF16)
    wqk = wb[:, 0:2 * MW]
    wvt = wb[:, 2 * MW:3 * MW].T
    wg = wb[:, 3 * MW:4 * MW]
    wd = wb[:, 4 * MW:4 * MW + 3 * DW]
    wz = wb[:, 4 * MW + 3 * DW:4 * MW + 4 * DW]
    ws = jnp.pad(wb[:, 4 * MW + 4 * DW:], ((0, 0), (0, LANES - 2 * DH)))
    alog = jnp.pad(a_log.astype(F32), (DH, LANES - 2 * DH)).reshape(1, LANES)
    dtb = jnp.pad(dt_bias.astype(F32), (DH, LANES - 2 * DH)).reshape(1, LANES)

    half = ROPE_D // 2
    inv_freq = jnp.power(ROPE_THETA, -jnp.arange(0, ROPE_D, 2, dtype=F32) / ROPE_D)
    ang = jnp.arange(S, dtype=jnp.int32).astype(F32)[:, None] * inv_freq[None, :]
    cos, sin = jnp.cos(ang), jnp.sin(ang)
    one = jnp.ones((S, MD - ROPE_D), F32)
    zero = jnp.zeros((S, MD - ROPE_D), F32)
    zh = jnp.zeros((S, half), F32)
    rc = jnp.tile(jnp.concatenate([cos, cos, one], axis=1), (1, 2))
    rs1 = jnp.tile(jnp.concatenate([-sin, zh, zero], axis=1), (1, 2))
    rs2 = jnp.tile(jnp.concatenate([zh, sin, zero], axis=1), (1, 2))

    const = lambda shape: pl.BlockSpec(shape, lambda b, t: (0,) * len(shape))
    row = lambda w: pl.BlockSpec((1, TM, w), lambda b, t: (b, t, 0))
    tab = pl.BlockSpec((TM, LANES), lambda b, t: (t, 0))
    out_shapes = [
        jax.ShapeDtypeStruct((B, S, MW), BF16),
        jax.ShapeDtypeStruct((B, S, MW), BF16),
        jax.ShapeDtypeStruct((B, MH, nb, S), F32),
        jax.ShapeDtypeStruct((B, nb, MW, BLK), BF16),
        jax.ShapeDtypeStruct((B, S, MW), F32),
        jax.ShapeDtypeStruct((B, S, 3 * DW), F32),
        jax.ShapeDtypeStruct((B, S, DW), F32),
        jax.ShapeDtypeStruct((B, S, LANES), F32),
        jax.ShapeDtypeStruct((B, S // CH, 8, CH), F32),
    ]
    out_specs = [
        row(MW), row(MW),
        pl.BlockSpec((1, MH, nb, TM), lambda b, t: (b, 0, 0, t)),
        pl.BlockSpec((1, 1, MW, BLK), lambda b, t: (b, t, 0, 0)),
        row(MW), row(3 * DW), row(DW), row(LANES),
        pl.BlockSpec((1, TM // CH, 8, CH), lambda b, t: (b, t, 0, 0)),
    ]
    return pl.pallas_call(
        functools.partial(_inproj_kernel, nb=nb),
        grid=(B, nt),
        in_specs=[row(D), pl.BlockSpec((1, 3, D), lambda b, t: (b, 0, 0)), const((1, D)),
                  const((D, 2 * MW)), const((MW, D)), const((D, MW)), const((D, 3 * DW)),
                  const((D, DW)), const((D, LANES)), tab, tab, tab,
                  const((1, LANES)), const((1, LANES))],
        out_specs=out_specs,
        out_shape=out_shapes,
        scratch_shapes=[pltpu.VMEM((nb, MW), F32)],
        compiler_params=pltpu.CompilerParams(
            dimension_semantics=("parallel", "arbitrary"), vmem_limit_bytes=VMEM_LIMIT),
        name="inproj",
    )(x, mod3, pre_w.reshape(1, D), wqk, wvt, wg, wd, wz, ws, rc, rs1, rs2, alog, dtb)


NPAIR = MW // LANES
ONES = 16


def _moba_kernel(q_ref, k_ref, vt_ref, bias_ref, sg_ref, o_ref,
                 pm_ref, s0_ref, s1_ref, qm_ref, acc_ref):
    i = pl.program_id(1)
    lane = lax.broadcasted_iota(jnp.int32, (1, LANES), 1)
    head_lanes = [lane < MD, lane >= MD]

    def fold8(a):
        return a.reshape(BLK // 8, 8, BLK)

    @pl.when(i == 0)
    def _():
        kpos = lax.broadcasted_iota(jnp.int32, (BLK, BLK), 0)
        qpos = lax.broadcasted_iota(jnp.int32, (BLK, BLK), 1)
        pm_ref[0] = jnp.zeros((BLK, BLK), F32)
        pm_ref[1] = jnp.where(kpos <= qpos, 0.0, NEG)
        pm_ref[2] = jnp.full((BLK, BLK), NEG, F32)

    ngroups = i // PVG + 1

    s_refs = (s0_ref, s1_ref)

    def scores(pr, qs, j, mx):
        kj = k_ref[0, pl.ds(pl.multiple_of(j * BLK, BLK), BLK), pr * LANES:(pr + 1) * LANES]
        pm = pm_ref[jnp.clip(j - i, -1, 1) + 1]
        out = []
        for h in range(2):
            s = lax.dot_general(kj, qs[h][...], NT, preferred_element_type=F32) + pm
            s_refs[pr % 2][h, j] = s
            out.append(jnp.maximum(mx[h], jnp.max(fold8(s), axis=0) + bias_ref[0, 2 * pr + h, pl.ds(j, 1), :]))
        return tuple(out)

    ones_rows = jnp.ones((ONES, BLK), BF16)

    def probs_pv(pr, m, j):
        for h in range(2):
            p = jnp.exp2(s_refs[pr % 2][h, j] - (m[h] - bias_ref[0, 2 * pr + h, pl.ds(j, 1), :]))
            vth = vt_ref[0, j, pr * LANES + h * MD:pr * LANES + (h + 1) * MD, :]
            acc_ref[h] += jnp.dot(jnp.concatenate([vth, ones_rows], axis=0), p.astype(BF16),
                                  preferred_element_type=F32)

    m = {}
    for stage in range(NPAIR + 1):
        p1 = stage if stage < NPAIR else None
        p2 = stage - 1 if stage >= 1 else None
        qs = None
        if p1 is not None:
            qp = q_ref[0, :, p1 * LANES:(p1 + 1) * LANES]
            for h in range(2):
                qm_ref[p1 % 2, h] = jnp.where(head_lanes[h], qp, jnp.zeros_like(qp))
            qs = [qm_ref.at[p1 % 2, h] for h in range(2)]
        if p2 is not None:
            acc_ref[...] = jnp.zeros_like(acc_ref)

        def body(t, mx):
            for dj in range(PVG):
                if p1 is not None:
                    mx = scores(p1, qs, t * PVG + dj, mx)
                if p2 is not None:
                    probs_pv(p2, m[p2], t * PVG + dj)
            return mx

        init = tuple(jnp.full((8, BLK), NEG, F32) for _ in range(2)) if p1 is not None else ()
        mx = lax.fori_loop(0, ngroups, body, init)
        if p1 is not None:
            m[p1] = [jnp.max(mx[h], axis=0, keepdims=True) for h in range(2)]
        if p2 is not None:
            ot = jnp.concatenate([acc_ref[h, 0:MD, :] / acc_ref[h, MD:MD + 1, :] for h in range(2)],
                                 axis=0)
            sl = slice(p2 * LANES, (p2 + 1) * LANES)
            o_ref[0, :, sl] = (ot.T * sg_ref[0, :, sl]).astype(o_ref.dtype)


def _moba(q, k, vt, bias, sg):
    B, S, _ = q.shape
    nb = S // BLK
    return pl.pallas_call(
        _moba_kernel,
        grid=(B, nb),
        in_specs=[pl.BlockSpec((1, BLK, MW), lambda b, i: (b, i, 0)),
                  pl.BlockSpec((1, S, MW), lambda b, i: (b, 0, 0)),
                  pl.BlockSpec((1, nb, MW, BLK), lambda b, i: (b, 0, 0, 0)),
                  pl.BlockSpec((1, MH, nb, BLK), lambda b, i: (b, 0, 0, i)),
                  pl.BlockSpec((1, BLK, MW), lambda b, i: (b, i, 0))],
        out_specs=pl.BlockSpec((1, BLK, MW), lambda b, i: (b, i, 0)),
        out_shape=jax.ShapeDtypeStruct((B, S, MW), BF16),
        scratch_shapes=[pltpu.VMEM((3, BLK, BLK), F32),
                        pltpu.VMEM((2, nb, BLK, BLK), F32),
                        pltpu.VMEM((2, nb, BLK, BLK), F32),
                        pltpu.VMEM((2, 2, BLK, LANES), BF16),
                        pltpu.VMEM((2, MD + ONES, BLK), F32)],
        compiler_params=pltpu.CompilerParams(
            dimension_semantics=("parallel", "arbitrary"), vmem_limit_bytes=VMEM_LIMIT),
        name="moba",
    )(q, k, vt, bias, sg)


def _dn_kernel(x_ref, sz_ref, gb_ref, gbt_ref, cw_ref, nw_ref, o_ref, xbuf, qkv, state, mqs, cs, os_, gls):
    t = pl.program_id(1)
    halo = 8

    @pl.when(t == 0)
    def _():
        xbuf[0:halo, :] = jnp.zeros((halo, 3 * DW), F32)
        state[...] = jnp.zeros_like(state)

    @pl.when(t > 0)
    def _():
        xbuf[0:halo, :] = xbuf[TD:TD + halo, :]

    xbuf[halo:halo + TD, :] = x_ref[0]

    for n in range(3 * DH):
        sl = slice(n * DK, (n + 1) * DK)
        y = jnp.zeros((TD, DK), F32)
        for tap in range(CONV):
            lo = halo - (CONV - 1) + tap
            y = y + cw_ref[tap:tap + 1, sl] * xbuf[lo:lo + TD, sl]
        y = _silu(y)
        if n < 2 * DH:
            y = y * lax.rsqrt(jnp.sum(y * y, axis=-1, keepdims=True) + EPS)
        if n < DH:
            y = y * (DK ** -0.5)
        qkv[:, sl] = y

    r = lax.broadcasted_iota(jnp.int32, (CH, CH), 0)
    c = lax.broadcasted_iota(jnp.int32, (CH, CH), 1)
    tril = r >= c
    r2 = lax.broadcasted_iota(jnp.int32, (CH, 2 * CH), 0)
    c2 = lax.broadcasted_iota(jnp.int32, (CH, 2 * CH), 1)
    c4 = lax.broadcasted_iota(jnp.int32, (CH, 2 * DK), 1)
    keep = jnp.logical_or(jnp.logical_and(c2 < CH, c2 >= r2), c2 - CH > r2)
    ones_l3 = jnp.concatenate([tril.astype(F32)] * 3, axis=1).astype(BF16)
    ones_u3 = jnp.concatenate([(r <= c).astype(F32)] * 3, axis=0).astype(BF16)
    nw = nw_ref[...]

    def local(it, _):
        units = [(dc, h) for dc in range(CU) for h in range(DH)]
        ci = [it * CU + dc for dc in range(CU)]
        r0 = [pl.multiple_of(c_ * CH, CH) for c_ in ci]
        gcol = [gb_ref[0, pl.ds(r0[dc], CH), :] for dc in range(CU)]
        grow = [gbt_ref[0, ci[dc]] for dc in range(CU)]
        gc = [jnp.dot(ones_l3, jnp.concatenate(_split3_bf16(g), axis=0).astype(BF16),
                      preferred_element_type=F32) for g in gcol]
        gr = [jnp.dot(jnp.concatenate(_split3_bf16(g), axis=1).astype(BF16), ones_u3,
                      preferred_element_type=F32) for g in grow]
        q = [qkv[pl.ds(r0[dc], CH), h * DK:(h + 1) * DK] for dc, h in units]
        k = [qkv[pl.ds(r0[dc], CH), DW + h * DK:DW + (h + 1) * DK] for dc, h in units]
        v = [qkv[pl.ds(r0[dc], CH), 2 * DW + h * DK:2 * DW + (h + 1) * DK] for dc, h in units]
        beta = [gcol[dc][:, h:h + 1] for dc, h in units]
        gcum = [gc[dc][:, DH + h:DH + h + 1] for dc, h in units]
        glast = [g[CH - 1:CH, :] for g in gcum]
        us = range(len(units))
        eg = [jnp.exp(g) for g in gcum]
        kb = [k[u] * beta[u] for u in us]
        dT = []
        for u, (dc, h) in enumerate(units):
            g2 = jnp.concatenate([gr[dc][DH + h:DH + h + 1, :]] * 2, axis=1)
            dT.append(jnp.where(keep, jnp.exp(jnp.where(keep, g2 - gcum[u], 0.0)), 0.0))
        aqt = [lax.dot_general(k[u].astype(BF16), jnp.concatenate([q[u], kb[u]], axis=0).astype(BF16), NT,
                               preferred_element_type=F32) for u in us]
        w = [jnp.concatenate([k[u] * jnp.exp(glast[u] - gcum[u]), aqt[u] * dT[u]], axis=1) for u in us]
        for lvl in range(6):
            for u in us:
                wh, wlo = _split_bf16(w[u])
                ph = pltpu.roll(wh[:, DK:], CH, axis=1)
                lhs = jnp.concatenate([jnp.where(c2 < CH, ph, wlo[:, DK:]), ph[:, :CH]], axis=1)
                res = jnp.dot(lhs.astype(BF16), jnp.concatenate([wh, wh, wlo], axis=0).astype(BF16),
                              preferred_element_type=F32)
                w[u] = jnp.where(c4 < 3 * CH, w[u] - res if lvl == 0 else w[u] + res, res)
        x0 = [jnp.concatenate([v[u] * beta[u], kb[u] * eg[u]], axis=1).astype(BF16) for u in us]
        rr = [jnp.dot(w[u].T.astype(BF16), x0[u], preferred_element_type=F32) for u in us]
        for dc in range(CU):
            gls[ci[dc]] = jnp.exp(gc[dc][CH - 1:CH, :])
        for u, (dc, h) in enumerate(units):
            mqs[h, ci[dc], 0:DK, :] = rr[u][0:DK, DK:].astype(BF16)
            mqs[h, ci[dc], DK:DK + CH, :] = (q[u] * eg[u] - rr[u][DK:DK + CH, DK:]).astype(BF16)
            cs[h, ci[dc]] = rr[u][0:DK, :DK]
            os_[pl.ds(r0[dc], CH), h * DK:(h + 1) * DK] = rr[u][DK:DK + CH, :DK]
        return 0

    lax.fori_loop(0, TD // (CH * CU), local, 0)

    for ci in range(TD // CH):
        r0 = ci * CH
        gl_all = gls[ci]
        for h in range(DH):
            st = state[h]
            rr = jnp.dot(mqs[h, ci], st.astype(BF16), preferred_element_type=F32)
            state[h] = st * gl_all[:, DH + h:DH + h + 1] - rr[:DK] + cs[h, ci]
            o = rr[DK:] + os_[r0:r0 + CH, h * DK:(h + 1) * DK]
            o = o * lax.rsqrt(jnp.mean(o * o, axis=-1, keepdims=True) + EPS) * nw
            o = o * sz_ref[0, r0:r0 + CH, h * DK:(h + 1) * DK]
            o_ref[0, r0:r0 + CH, h * DK:(h + 1) * DK] = o.astype(o_ref.dtype)


def _deltanet(dx, sz, gb, gbt, conv_w, dn_norm_w):
    B, S, _ = dx.shape
    nt = S // TD
    return pl.pallas_call(
        _dn_kernel,
        grid=(B, nt),
        in_specs=[pl.BlockSpec((1, TD, 3 * DW), lambda b, t: (b, t, 0)),
                  pl.BlockSpec((1, TD, DW), lambda b, t: (b, t, 0)),
                  pl.BlockSpec((1, TD, LANES), lambda b, t: (b, t, 0)),
                  pl.BlockSpec((1, TD // CH, 8, CH), lambda b, t: (b, t, 0, 0)),
                  pl.BlockSpec((CONV, 3 * DW), lambda b, t: (0, 0)),
                  pl.BlockSpec((1, DK), lambda b, t: (0, 0))],
        out_specs=pl.BlockSpec((1, TD, DW), lambda b, t: (b, t, 0)),
        out_shape=jax.ShapeDtypeStruct((B, S, DW), BF16),
        scratch_shapes=[pltpu.VMEM((TD + 8, 3 * DW), F32),
                        pltpu.VMEM((TD, 3 * DW), F32),
                        pltpu.VMEM((DH, DK, DK), F32),
                        pltpu.VMEM((DH, TD // CH, DK + CH, DK), BF16),
                        pltpu.VMEM((DH, TD // CH, DK, DK), F32),
                        pltpu.VMEM((TD, DW), F32),
                        pltpu.VMEM((TD // CH, 1, LANES), F32)],
        compiler_params=pltpu.CompilerParams(
            dimension_semantics=("parallel", "arbitrary"), vmem_limit_bytes=VMEM_LIMIT),
        name="deltanet",
    )(dx, sz, gb, gbt, conv_w, dn_norm_w.reshape(1, DK))


def _outproj_kernel(om_ref, od_ref, x_ref, mod_ref, pw_ref, wt_ref, wb_ref, o_ref):
    y = jnp.dot(om_ref[0], wt_ref[...], preferred_element_type=F32)
    y = y + jnp.dot(od_ref[0], wb_ref[...], preferred_element_type=F32)
    yn = y * lax.rsqrt(jnp.mean(y * y, axis=-1, keepdims=True) + EPS) * pw_ref[...]
    o_ref[0] = x_ref[0] + mod_ref[0, 2:3, :] * yn


def _outproj(om, od, x, mod3, post_w, w_out):
    B, S, _ = x.shape
    wb = w_out.astype(BF16)
    row = lambda w: pl.BlockSpec((1, TO, w), lambda b, t: (b, t, 0))
    return pl.pallas_call(
        _outproj_kernel,
        grid=(B, S // TO),
        in_specs=[row(MW), row(DW), row(D),
                  pl.BlockSpec((1, 3, D), lambda b, t: (b, 0, 0)),
                  pl.BlockSpec((1, D), lambda b, t: (0, 0)),
                  pl.BlockSpec((MW, D), lambda b, t: (0, 0)),
                  pl.BlockSpec((DW, D), lambda b, t: (0, 0))],
        out_specs=row(D),
        out_shape=jax.ShapeDtypeStruct((B, S, D), F32),
        compiler_params=pltpu.CompilerParams(
            dimension_semantics=("parallel", "parallel"), vmem_limit_bytes=VMEM_LIMIT),
        name="outproj",
    )(om, od, x, mod3, post_w.reshape(1, D), wb[:MW], wb[MW:])


def kernel(x, c, ada_w, ada_b, pre_norm_w, post_norm_w, w_in, conv_w, a_log, dt_bias, dn_norm_w, w_out):
    B, S, _ = x.shape
    assert S % TD == 0 and S % TO == 0 and S % (BLK * PVG) == 0 and TM == BLK
    depth = ada_w.shape[0]
    for l in range(depth):
        mod3 = _mod(c, ada_w[l], ada_b[l]).reshape(B, 3, D)
        q, k, bias, vt, sg, dx, sz, gb, gbt = _inproj(x, mod3, pre_norm_w[l], w_in[l], a_log[l], dt_bias[l])
        om = _moba(q, k, vt, bias, sg)
        od = _deltanet(dx, sz, gb, gbt, conv_w[l], dn_norm_w[l])
        x = _outproj(om, od, x, mod3, post_norm_w[l], w_out[l])
    return x
```

```python
import functools
import math

import jax
import jax.numpy as jnp
from jax import lax
from jax.experimental import pallas as pl
from jax.experimental.pallas import tpu as pltpu

F32 = jnp.float32
BF16 = jnp.bfloat16
HI = lax.Precision.HIGHEST

D = 1024
MH, MD = 8, 64
MW = MH * MD
BLK = 256
TOPK = 3
ROPE_D = 16
ROPE_THETA = 500000.0
DH, DK = 4, 128
DW = DH * DK
CONV = 4
CH = 64
EPS = 1e-6
NEG = -1e30
LOWEST = -3.0e38
LOG2E = math.log2(math.e)
PVG = 4

LANES = 128
VMEM_LIMIT = 56 * 1024 * 1024

TM = 256
TD = 512
CU = 8
TO = 1024

NT = (((1,), (1,)), ((), ()))
TN = (((0,), (0,)), ((), ()))


def _silu(v):
    h = 0.5 * v
    return h + h * jnp.tanh(h)


def _split_bf16(p):
    hi = p.astype(BF16).astype(F32)
    lo = (p - hi).astype(BF16).astype(F32)
    return hi, lo


def _split3_bf16(p):
    a, r = _split_bf16(p)
    c = (p - a - r).astype(BF16).astype(F32)
    return a, r, c


def _mod_kernel(c_ref, w_ref, b_ref, o_ref):
    c = c_ref[...]
    o_ref[...] = jnp.dot(_silu(c), w_ref[...], precision=HI, preferred_element_type=F32) + b_ref[...]


def _mod(c, ada_w, ada_b):
    B = c.shape[0]
    tn = 512
    return pl.pallas_call(
        _mod_kernel,
        grid=(3 * D // tn,),
        in_specs=[pl.BlockSpec((B, D), lambda j: (0, 0)),
                  pl.BlockSpec((D, tn), lambda j: (0, j)),
                  pl.BlockSpec((1, tn), lambda j: (0, j))],
        out_specs=pl.BlockSpec((B, tn), lambda j: (0, j)),
        out_shape=jax.ShapeDtypeStruct((B, 3 * D), F32),
        name="mod",
    )(c, ada_w, ada_b.reshape(1, 3 * D))


def _inproj_kernel(x_ref, mod_ref, pw_ref, wqk_ref, wvt_ref, wg_ref, wd_ref, wz_ref, ws_ref,
                   rc_ref, rs1_ref, rs2_ref, alog_ref, dtb_ref,
                   q_ref, k_ref, bias_ref, vt_ref, sg_ref, dx_ref, sz_ref, gb_ref, gbt_ref, kms, *, nb):
    t = pl.program_id(1)
    x = x_ref[0]
    shift = mod_ref[0, 0:1, :]
    scale = mod_ref[0, 1:2, :]
    y = x * lax.rsqrt(jnp.mean(x * x, axis=-1, keepdims=True) + EPS) * pw_ref[...]
    h = (y * (1.0 + scale) + shift).astype(BF16)

    rc, rs1, rs2 = rc_ref[...], rs1_ref[...], rs2_ref[...]

    def rope(t):
        tm1 = pltpu.roll(t, MW - ROPE_D // 2, axis=1)
        tp1 = pltpu.roll(t, ROPE_D // 2, axis=1)
        parts = []
        for n in range(MW // LANES):
            sl = slice(n * LANES, (n + 1) * LANES)
            parts.append(t[:, sl] * rc + tm1[:, sl] * rs1 + tp1[:, sl] * rs2)
        return jnp.concatenate(parts, axis=1)

    qk = jnp.dot(h, wqk_ref[...], preferred_element_type=F32)
    q = rope(qk[:, :MW])
    k = rope(qk[:, MW:])
    k_ref[0] = k.astype(BF16)
    q_ref[0] = (q * (MD ** -0.5 * LOG2E)).astype(BF16)

    vt = lax.dot_general(wvt_ref[...], h, NT, preferred_element_type=F32)
    vt_ref[0, 0] = vt.astype(BF16)

    sg_ref[0] = _silu(jnp.dot(h, wg_ref[...], preferred_element_type=F32))

    @pl.when(t == 0)
    def _():
        kms[...] = jnp.zeros_like(kms)

    km = kms[...]
    wl = lax.broadcasted_iota(jnp.int32, (1, MW), 1)
    kmh = jnp.concatenate([jnp.where((wl >= hh * MD) & (wl < (hh + 1) * MD), km, 0.0) for hh in range(MH)], axis=0)
    kh, kl = _split_bf16(kmh)
    qh, ql = _split_bf16(q)
    gate = lax.dot_general(jnp.concatenate([kh, kl, kh], axis=1).astype(BF16),
                           jnp.concatenate([qh, qh, ql], axis=1).astype(BF16), NT,
                           preferred_element_type=F32)
    blk = lax.broadcasted_iota(jnp.int32, (MH, nb, TM), 1)
    past = blk < t
    g3 = jnp.where(past, gate.reshape(MH, nb, TM), NEG)
    sel = jnp.zeros((MH, nb, TM), jnp.bool_)
    for _ in range(min(TOPK, nb)):
        gmax = jnp.max(g3, axis=1, keepdims=True)
        idx = jnp.min(jnp.where(g3 == gmax, blk, nb), axis=1, keepdims=True)
        pick = blk == idx
        sel = jnp.logical_or(sel, pick)
        g3 = jnp.where(pick, LOWEST, g3)
    bias_ref[0] = jnp.where(jnp.logical_or(jnp.logical_and(sel, past), blk == t), 0.0, NEG)
    kms[pl.ds(t, 1), :] = jnp.mean(k, axis=0, keepdims=True)

    s = jnp.dot(h, ws_ref[...], preferred_element_type=F32)
    lane = lax.broadcasted_iota(jnp.int32, s.shape, 1)
    beta = jax.nn.sigmoid(s)
    z = s + dtb_ref[...]
    softplus = jnp.maximum(z, 0.0) + jnp.log1p(jnp.exp(-jnp.abs(z)))
    g = -jnp.exp(alog_ref[...]) * softplus
    gb = jnp.where(lane < DH, beta, jnp.where(lane < 2 * DH, g, 0.0))
    gb_ref[0] = gb
    gbt = gb.T[0:8, :]
    for cidx in range(TM // CH):
        gbt_ref[0, cidx] = gbt[:, cidx * CH:(cidx + 1) * CH]

    sz_ref[0] = _silu(jnp.dot(h, wz_ref[...], preferred_element_type=F32))
    dx_ref[0] = jnp.dot(h, wd_ref[...], preferred_element_type=F32)


def _inproj(x, mod3, pre_w, w_in, a_log, dt_bias):
    B, S, _ = x.shape
    nb = S // BLK
    nt = S // TM
    wb = w_in.astype(BF16)
    wqk = wb[:, 0:2 * MW]
    wvt = wb[:, 2 * MW:3 * MW].T
    wg = wb[:, 3 * MW:4 * MW]
    wd = wb[:, 4 * MW:4 * MW + 3 * DW]
    wz = wb[:, 4 * MW + 3 * DW:4 * MW + 4 * DW]
    ws = jnp.pad(wb[:, 4 * MW + 4 * DW:], ((0, 0), (0, LANES - 2 * DH)))
    alog = jnp.pad(a_log.astype(F32), (DH, LANES - 2 * DH)).reshape(1, LANES)
    dtb = jnp.pad(dt_bias.astype(F32), (DH, LANES - 2 * DH)).reshape(1, LANES)

    half = ROPE_D // 2
    inv_freq = jnp.power(ROPE_THETA, -jnp.arange(0, ROPE_D, 2, dtype=F32) / ROPE_D)
    ang = jnp.arange(S, dtype=jnp.int32).astype(F32)[:, None] * inv_freq[None, :]
    cos, sin = jnp.cos(ang), jnp.sin(ang)
    one = jnp.ones((S, MD - ROPE_D), F32)
    zero = jnp.zeros((S, MD - ROPE_D), F32)
    zh = jnp.zeros((S, half), F32)
    rc = jnp.tile(jnp.concatenate([cos, cos, one], axis=1), (1, 2))
    rs1 = jnp.tile(jnp.concatenate([-sin, zh, zero], axis=1), (1, 2))
    rs2 = jnp.tile(jnp.concatenate([zh, sin, zero], axis=1), (1, 2))

    const = lambda shape: pl.BlockSpec(shape, lambda b, t: (0,) * len(shape))
    row = lambda w: pl.BlockSpec((1, TM, w), lambda b, t: (b, t, 0))
    tab = pl.BlockSpec((TM, LANES), lambda b, t: (t, 0))
    out_shapes = [
        jax.ShapeDtypeStruct((B, S, MW), BF16),
        jax.ShapeDtypeStruct((B, S, MW), BF16),
        jax.ShapeDtypeStruct((B, MH, nb, S), F32),
        jax.ShapeDtypeStruct((B, nb, MW, BLK), BF16),
        jax.ShapeDtypeStruct((B, S, MW), F32),
        jax.ShapeDtypeStruct((B, S, 3 * DW), F32),
        jax.ShapeDtypeStruct((B, S, DW), F32),
        jax.ShapeDtypeStruct((B, S, LANES), F32),
        jax.ShapeDtypeStruct((B, S // CH, 8, CH), F32),
    ]
    out_specs = [
        row(MW), row(MW),
        pl.BlockSpec((1, MH, nb, TM), lambda b, t: (b, 0, 0, t)),
        pl.BlockSpec((1, 1, MW, BLK), lambda b, t: (b, t, 0, 0)),
        row(MW), row(3 * DW), row(DW), row(LANES),
        pl.BlockSpec((1, TM // CH, 8, CH), lambda b, t: (b, t, 0, 0)),
    ]
    return pl.pallas_call(
        functools.partial(_inproj_kernel, nb=nb),
        grid=(B, nt),
        in_specs=[row(D), pl.BlockSpec((1, 3, D), lambda b, t: (b, 0, 0)), const((1, D)),
                  const((D, 2 * MW)), const((MW, D)), const((D, MW)), const((D, 3 * DW)),
                  const((D, DW)), const((D, LANES)), tab, tab, tab,
                  const((1, LANES)), const((1, LANES))],
        out_specs=out_specs,
        out_shape=out_shapes,
        scratch_shapes=[pltpu.VMEM((nb, MW), F32)],
        compiler_params=pltpu.CompilerParams(
            dimension_semantics=("parallel", "arbitrary"), vmem_limit_bytes=VMEM_LIMIT),
        name="inproj",
    )(x, mod3, pre_w.reshape(1, D), wqk, wvt, wg, wd, wz, ws, rc, rs1, rs2, alog, dtb)


NPAIR = MW // LANES
ONES = 16


def _moba_kernel(q_ref, k_ref, vt_ref, bias_ref, sg_ref, o_ref,
                 pm_ref, s0_ref, s1_ref, qm_ref, acc_ref):
    i = pl.program_id(1)
    lane = lax.broadcasted_iota(jnp.int32, (1, LANES), 1)
    head_lanes = [lane < MD, lane >= MD]

    def fold8(a):
        return a.reshape(BLK // 8, 8, BLK)

    @pl.when(i == 0)
    def _():
        kpos = lax.broadcasted_iota(jnp.int32, (BLK, BLK), 0)
        qpos = lax.broadcasted_iota(jnp.int32, (BLK, BLK), 1)
        pm_ref[0] = jnp.zeros((BLK, BLK), F32)
        pm_ref[1] = jnp.where(kpos <= qpos, 0.0, NEG)
        pm_ref[2] = jnp.full((BLK, BLK), NEG, F32)

    ngroups = i // PVG + 1

    s_refs = (s0_ref, s1_ref)

    def scores(pr, qs, j, mx):
        kj = k_ref[0, pl.ds(pl.multiple_of(j * BLK, BLK), BLK), pr * LANES:(pr + 1) * LANES]
        pm = pm_ref[jnp.clip(j - i, -1, 1) + 1]
        out = []
        for h in range(2):
            s = lax.dot_general(kj, qs[h][...], NT, preferred_element_type=F32) + pm
            s_refs[pr % 2][h, j] = s
            out.append(jnp.maximum(mx[h], jnp.max(fold8(s), axis=0) + bias_ref[0, 2 * pr + h, pl.ds(j, 1), :]))
        return tuple(out)

    ones_rows = jnp.ones((ONES, BLK), BF16)

    def probs_pv(pr, m, j):
        for h in range(2):
            p = jnp.exp2(s_refs[pr % 2][h, j] - (m[h] - bias_ref[0, 2 * pr + h, pl.ds(j, 1), :]))
            vth = vt_ref[0, j, pr * LANES + h * MD:pr * LANES + (h + 1) * MD, :]
            acc_ref[h] += jnp.dot(jnp.concatenate([vth, ones_rows], axis=0), p.astype(BF16),
                                  preferred_element_type=F32)

    m = {}
    for stage in range(NPAIR + 1):
        p1 = stage if stage < NPAIR else None
        p2 = stage - 1 if stage >= 1 else None
        qs = None
        if p1 is not None:
            qp = q_ref[0, :, p1 * LANES:(p1 + 1) * LANES]
            for h in range(2):
                qm_ref[p1 % 2, h] = jnp.where(head_lanes[h], qp, jnp.zeros_like(qp))
            qs = [qm_ref.at[p1 % 2, h] for h in range(2)]
        if p2 is not None:
            acc_ref[...] = jnp.zeros_like(acc_ref)

        def body(t, mx):
            for dj in range(PVG):
                if p1 is not None:
                    mx = scores(p1, qs, t * PVG + dj, mx)
                if p2 is not None:
                    probs_pv(p2, m[p2], t * PVG + dj)
            return mx

        init = tuple(jnp.full((8, BLK), NEG, F32) for _ in range(2)) if p1 is not None else ()
        mx = lax.fori_loop(0, ngroups, body, init)
        if p1 is not None:
            m[p1] = [jnp.max(mx[h], axis=0, keepdims=True) for h in range(2)]
        if p2 is not None:
            ot = jnp.concatenate([acc_ref[h, 0:MD, :] / acc_ref[h, MD:MD + 1, :] for h in range(2)],
                                 axis=0)
            sl = slice(p2 * LANES, (p2 + 1) * LANES)
            o_ref[0, :, sl] = (ot.T * sg_ref[0, :, sl]).astype(o_ref.dtype)


def _moba(q, k, vt, bias, sg):
    B, S, _ = q.shape
    nb = S // BLK
    return pl.pallas_call(
        _moba_kernel,
        grid=(B, nb),
        in_specs=[pl.BlockSpec((1, BLK, MW), lambda b, i: (b, i, 0)),
                  pl.BlockSpec((1, S, MW), lambda b, i: (b, 0, 0)),
                  pl.BlockSpec((1, nb, MW, BLK), lambda b, i: (b, 0, 0, 0)),
                  pl.BlockSpec((1, MH, nb, BLK), lambda b, i: (b, 0, 0, i)),
                  pl.BlockSpec((1, BLK, MW), lambda b, i: (b, i, 0))],
        out_specs=pl.BlockSpec((1, BLK, MW), lambda b, i: (b, i, 0)),
        out_shape=jax.ShapeDtypeStruct((B, S, MW), BF16),
        scratch_shapes=[pltpu.VMEM((3, BLK, BLK), F32),
                        pltpu.VMEM((2, nb, BLK, BLK), F32),
                        pltpu.VMEM((2, nb, BLK, BLK), F32),
                        pltpu.VMEM((2, 2, BLK, LANES), BF16),
                        pltpu.VMEM((2, MD + ONES, BLK), F32)],
        compiler_params=pltpu.CompilerParams(
            dimension_semantics=("parallel", "arbitrary"), vmem_limit_bytes=VMEM_LIMIT),
        name="moba",
    )(q, k, vt, bias, sg)


def _dn_kernel(x_ref, sz_ref, gb_ref, gbt_ref, cw_ref, nw_ref, o_ref, xbuf, qkv, state, mqs, cs, os_, gls):
    t = pl.program_id(1)
    halo = 8

    @pl.when(t == 0)
    def _():
        xbuf[0:halo, :] = jnp.zeros((halo, 3 * DW), F32)
        state[...] = jnp.zeros_like(state)

    @pl.when(t > 0)
    def _():
        xbuf[0:halo, :] = xbuf[TD:TD + halo, :]

    xbuf[halo:halo + TD, :] = x_ref[0]

    for n in range(3 * DH):
        sl = slice(n * DK, (n + 1) * DK)
        y = jnp.zeros((TD, DK), F32)
        for tap in range(CONV):
            lo = halo - (CONV - 1) + tap
            y = y + cw_ref[tap:tap + 1, sl] * xbuf[lo:lo + TD, sl]
        y = _silu(y)
        if n < 2 * DH:
            y = y * lax.rsqrt(jnp.sum(y * y, axis=-1, keepdims=True) + EPS)
        if n < DH:
            y = y * (DK ** -0.5)
        qkv[:, sl] = y

    r = lax.broadcasted_iota(jnp.int32, (CH, CH), 0)
    c = lax.broadcasted_iota(jnp.int32, (CH, CH), 1)
    tril = r >= c
    r2 = lax.broadcasted_iota(jnp.int32, (CH, 2 * CH), 0)
    c2 = lax.broadcasted_iota(jnp.int32, (CH, 2 * CH), 1)
    c4 = lax.broadcasted_iota(jnp.int32, (CH, 2 * DK), 1)
    keep = jnp.logical_or(jnp.logical_and(c2 < CH, c2 >= r2), c2 - CH > r2)
    ones_l3 = jnp.concatenate([tril.astype(F32)] * 3, axis=1).astype(BF16)
    ones_u3 = jnp.concatenate([(r <= c).astype(F32)] * 3, axis=0).astype(BF16)
    nw = nw_ref[...]

    def local(it, _):
        units = [(dc, h) for dc in range(CU) for h in range(DH)]
        ci = [it * CU + dc for dc in range(CU)]
        r0 = [pl.multiple_of(c_ * CH, CH) for c_ in ci]
        gcol = [gb_ref[0, pl.ds(r0[dc], CH), :] for dc in range(CU)]
        grow = [gbt_ref[0, ci[dc]] for dc in range(CU)]
        gc = [jnp.dot(ones_l3, jnp.concatenate(_split3_bf16(g), axis=0).astype(BF16),
                      preferred_element_type=F32) for g in gcol]
        gr = [jnp.dot(jnp.concatenate(_split3_bf16(g), axis=1).astype(BF16), ones_u3,
                      preferred_element_type=F32) for g in grow]
        q = [qkv[pl.ds(r0[dc], CH), h * DK:(h + 1) * DK] for dc, h in units]
        k = [qkv[pl.ds(r0[dc], CH), DW + h * DK:DW + (h + 1) * DK] for dc, h in units]
        v = [qkv[pl.ds(r0[dc], CH), 2 * DW + h * DK:2 * DW + (h + 1) * DK] for dc, h in units]
        beta = [gcol[dc][:, h:h + 1] for dc, h in units]
        gcum = [gc[dc][:, DH + h:DH + h + 1] for dc, h in units]
        glast = [g[CH - 1:CH, :] for g in gcum]
        us = range(len(units))
        eg = [jnp.exp(g) for g in gcum]
        kb = [k[u] * beta[u] for u in us]
        dT = []
        for u, (dc, h) in enumerate(units):
            g2 = jnp.concatenate([gr[dc][DH + h:DH + h + 1, :]] * 2, axis=1)
            dT.append(jnp.where(keep, jnp.exp(jnp.where(keep, g2 - gcum[u], 0.0)), 0.0))
        aqt = [lax.dot_general(k[u].astype(BF16), jnp.concatenate([q[u], kb[u]], axis=0).astype(BF16), NT,
                               preferred_element_type=F32) for u in us]
        w = [jnp.concatenate([k[u] * jnp.exp(glast[u] - gcum[u]), aqt[u] * dT[u]], axis=1) for u in us]
        for lvl in range(6):
            for u in us:
                wh, wlo = _split_bf16(w[u])
                ph = pltpu.roll(wh[:, DK:], CH, axis=1)
                lhs = jnp.concatenate([jnp.where(c2 < CH, ph, wlo[:, DK:]), ph[:, :CH]], axis=1)
                res = jnp.dot(lhs.astype(BF16), jnp.concatenate([wh, wh, wlo], axis=0).astype(BF16),
                              preferred_element_type=F32)
                w[u] = jnp.where(c4 < 3 * CH, w[u] - res if lvl == 0 else w[u] + res, res)
        x0 = [jnp.concatenate([v[u] * beta[u], kb[u] * eg[u]], axis=1).astype(BF16) for u in us]
        rr = [jnp.dot(w[u].T.astype(BF16), x0[u], preferred_element_type=F32) for u in us]
        for dc in range(CU):
            gls[ci[dc]] = jnp.exp(gc[dc][CH - 1:CH, :])
        for u, (dc, h) in enumerate(units):
            mqs[h, ci[dc], 0:DK, :] = rr[u][0:DK, DK:].astype(BF16)
            mqs[h, ci[dc], DK:DK + CH, :] = (q[u] * eg[u] - rr[u][DK:DK + CH, DK:]).astype(BF16)
            cs[h, ci[dc]] = rr[u][0:DK, :DK]
            os_[pl.ds(r0[dc], CH), h * DK:(h + 1) * DK] = rr[u][DK:DK + CH, :DK]
        return 0

    lax.fori_loop(0, TD // (CH * CU), local, 0)

    for ci in range(TD // CH):
        r0 = ci * CH
        gl_all = gls[ci]
        for h in range(DH):
            st = state[h]
            rr = jnp.dot(mqs[h, ci], st.astype(BF16), preferred_element_type=F32)
            state[h] = st * gl_all[:, DH + h:DH + h + 1] - rr[:DK] + cs[h, ci]
            o = rr[DK:] + os_[r0:r0 + CH, h * DK:(h + 1) * DK]
            o = o * lax.rsqrt(jnp.mean(o * o, axis=-1, keepdims=True) + EPS) * nw
            o = o * sz_ref[0, r0:r0 + CH, h * DK:(h + 1) * DK]
            o_ref[0, r0:r0 + CH, h * DK:(h + 1) * DK] = o.astype(o_ref.dtype)


def _deltanet(dx, sz, gb, gbt, conv_w, dn_norm_w):
    B, S, _ = dx.shape
    nt = S // TD
    return pl.pallas_call(
        _dn_kernel,
        grid=(B, nt),
        in_specs=[pl.BlockSpec((1, TD, 3 * DW), lambda b, t: (b, t, 0)),
                  pl.BlockSpec((1, TD, DW), lambda b, t: (b, t, 0)),
                  pl.BlockSpec((1, TD, LANES), lambda b, t: (b, t, 0)),
                  pl.BlockSpec((1, TD // CH, 8, CH), lambda b, t: (b, t, 0, 0)),
                  pl.BlockSpec((CONV, 3 * DW), lambda b, t: (0, 0)),
                  pl.BlockSpec((1, DK), lambda b, t: (0, 0))],
        out_specs=pl.BlockSpec((1, TD, DW), lambda b, t: (b, t, 0)),
        out_shape=jax.ShapeDtypeStruct((B, S, DW), BF16),
        scratch_shapes=[pltpu.VMEM((TD + 8, 3 * DW), F32),
                        pltpu.VMEM((TD, 3 * DW), F32),
                        pltpu.VMEM((DH, DK, DK), F32),
                        pltpu.VMEM((DH, TD // CH, DK + CH, DK), BF16),
                        pltpu.VMEM((DH, TD // CH, DK, DK), F32),
                        pltpu.VMEM((TD, DW), F32),
                        pltpu.VMEM((TD // CH, 1, LANES), F32)],
        compiler_params=pltpu.CompilerParams(
            dimension_semantics=("parallel", "arbitrary"), vmem_limit_bytes=VMEM_LIMIT),
        name="deltanet",
    )(dx, sz, gb, gbt, conv_w, dn_norm_w.reshape(1, DK))


def _outproj_kernel(om_ref, od_ref, x_ref, mod_ref, pw_ref, wt_ref, wb_ref, o_ref):
    y = jnp.dot(om_ref[0], wt_ref[...], preferred_element_type=F32)
    y = y + jnp.dot(od_ref[0], wb_ref[...], preferred_element_type=F32)
    yn = y * lax.rsqrt(jnp.mean(y * y, axis=-1, keepdims=True) + EPS) * pw_ref[...]
    o_ref[0] = x_ref[0] + mod_ref[0, 2:3, :] * yn


def _outproj(om, od, x, mod3, post_w, w_out):
    B, S, _ = x.shape
    wb = w_out.astype(BF16)
    row = lambda w: pl.BlockSpec((1, TO, w), lambda b, t: (b, t, 0))
    return pl.pallas_call(
        _outproj_kernel,
        grid=(B, S // TO),
        in_specs=[row(MW), row(DW), row(D),
                  pl.BlockSpec((1, 3, D), lambda b, t: (b, 0, 0)),
                  pl.BlockSpec((1, D), lambda b, t: (0, 0)),
                  pl.BlockSpec((MW, D), lambda b, t: (0, 0)),
                  pl.BlockSpec((DW, D), lambda b, t: (0, 0))],
        out_specs=row(D),
        out_shape=jax.ShapeDtypeStruct((B, S, D), F32),
        compiler_params=pltpu.CompilerParams(
            dimension_semantics=("parallel", "parallel"), vmem_limit_bytes=VMEM_LIMIT),
        name="outproj",
    )(om, od, x, mod3, post_w.reshape(1, D), wb[:MW], wb[MW:])


def kernel(x, c, ada_w, ada_b, pre_norm_w, post_norm_w, w_in, conv_w, a_log, dt_bias, dn_norm_w, w_out):
    B, S, _ = x.shape
    assert S % TD == 0 and S % TO == 0 and S % (BLK * PVG) == 0 and TM == BLK
    depth = ada_w.shape[0]
    for l in range(depth):
        mod3 = _mod(c, ada_w[l], ada_b[l]).reshape(B, 3, D)
        q, k, bias, vt, sg, dx, sz, gb, gbt = _inproj(x, mod3, pre_norm_w[l], w_in[l], a_log[l], dt_bias[l])
        om = _moba(q, k, vt, bias, sg)
        od = _deltanet(dx, sz, gb, gbt, conv_w[l], dn_norm_w[l])
        x = _outproj(om, od, x, mod3, post_norm_w[l], w_out[l])
    return x
```

```python
import functools
import math

import jax
import jax.numpy as jnp
from jax import lax
from jax.experimental import pallas as pl
from jax.experimental.pallas import tpu as pltpu

F32 = jnp.float32
BF16 = jnp.bfloat16
HI = lax.Precision.HIGHEST

D = 1024
MH, MD = 8, 64
MW = MH * MD
BLK = 256
TOPK = 3
ROPE_D = 16
ROPE_THETA = 500000.0
DH, DK = 4, 128
DW = DH * DK
CONV = 4
CH = 64
EPS = 1e-6
NEG = -1e30
LOWEST = -3.0e38
LOG2E = math.log2(math.e)
PVG = 4

LANES = 128
VMEM_LIMIT = 56 * 1024 * 1024

TM = 512
NBT = TM // BLK
TD = 512
CU = 8
TO = 1024

NT = (((1,), (1,)), ((), ()))
TN = (((0,), (0,)), ((), ()))


def _silu(v):
    h = 0.5 * v
    return h + h * jnp.tanh(h)


def _split_bf16(p):
    hi = p.astype(BF16).astype(F32)
    lo = (p - hi).astype(BF16).astype(F32)
    return hi, lo


def _split3_bf16(p):
    a, r = _split_bf16(p)
    c = (p - a - r).astype(BF16).astype(F32)
    return a, r, c


def _mod_kernel(c_ref, w_ref, b_ref, o_ref):
    c = c_ref[...]
    o_ref[...] = jnp.dot(_silu(c), w_ref[...], precision=HI, preferred_element_type=F32) + b_ref[...]


def _mod(c, ada_w, ada_b):
    B = c.shape[0]
    tn = 512
    return pl.pallas_call(
        _mod_kernel,
        grid=(3 * D // tn,),
        in_specs=[pl.BlockSpec((B, D), lambda j: (0, 0)),
                  pl.BlockSpec((D, tn), lambda j: (0, j)),
                  pl.BlockSpec((1, tn), lambda j: (0, j))],
        out_specs=pl.BlockSpec((B, tn), lambda j: (0, j)),
        out_shape=jax.ShapeDtypeStruct((B, 3 * D), F32),
        name="mod",
    )(c, ada_w, ada_b.reshape(1, 3 * D))


def _inproj_kernel(x_ref, mod_ref, pw_ref, wqk_ref, wvt_ref, wg_ref, wd_ref, wz_ref, ws_ref,
                   rc_ref, rs1_ref, rs2_ref, alog_ref, dtb_ref,
                   q_ref, k_ref, bias_ref, vt_ref, sg_ref, dx_ref, sz_ref, gb_ref, gbt_ref, kms, *, nb):
    t = pl.program_id(1)
    x = x_ref[0]
    shift = mod_ref[0, 0:1, :]
    scale = mod_ref[0, 1:2, :]
    y = x * lax.rsqrt(jnp.mean(x * x, axis=-1, keepdims=True) + EPS) * pw_ref[...]
    h = (y * (1.0 + scale) + shift).astype(BF16)

    rc, rs1, rs2 = rc_ref[...], rs1_ref[...], rs2_ref[...]

    def rope(t):
        tm1 = pltpu.roll(t, MW - ROPE_D // 2, axis=1)
        tp1 = pltpu.roll(t, ROPE_D // 2, axis=1)
        parts = []
        for n in range(MW // LANES):
            sl = slice(n * LANES, (n + 1) * LANES)
            parts.append(t[:, sl] * rc + tm1[:, sl] * rs1 + tp1[:, sl] * rs2)
        return jnp.concatenate(parts, axis=1)

    qk = jnp.dot(h, wqk_ref[...], preferred_element_type=F32)
    q = rope(qk[:, :MW])
    k = rope(qk[:, MW:])
    k_ref[0] = k.astype(BF16)
    q_ref[0] = (q * (MD ** -0.5 * LOG2E)).astype(BF16)

    vt = lax.dot_general(wvt_ref[...], h, NT, preferred_element_type=F32)
    for j in range(NBT):
        vt_ref[0, j] = vt[:, j * BLK:(j + 1) * BLK].astype(BF16)

    sg_ref[0] = _silu(jnp.dot(h, wg_ref[...], preferred_element_type=F32))

    @pl.when(t == 0)
    def _():
        kms[...] = jnp.zeros_like(kms)

    for j in range(NBT):
        kms[pl.ds(t * NBT + j, 1), :] = jnp.mean(k[j * BLK:(j + 1) * BLK], axis=0, keepdims=True)
    km = kms[...]
    wl = lax.broadcasted_iota(jnp.int32, (1, MW), 1)
    kmh = jnp.concatenate([jnp.where((wl >= hh * MD) & (wl < (hh + 1) * MD), km, 0.0) for hh in range(MH)], axis=0)
    kh, kl = _split_bf16(kmh)
    qh, ql = _split_bf16(q)
    gate = lax.dot_general(jnp.concatenate([kh, kl, kh], axis=1).astype(BF16),
                           jnp.concatenate([qh, qh, ql], axis=1).astype(BF16), NT,
                           preferred_element_type=F32)
    blk = lax.broadcasted_iota(jnp.int32, (MH, nb, TM), 1)
    qpos = lax.broadcasted_iota(jnp.int32, (MH, nb, TM), 2)
    own = t * NBT + sum((qpos >= j * BLK).astype(jnp.int32) for j in range(1, NBT))
    past = blk < own
    g3 = jnp.where(past, gate.reshape(MH, nb, TM), NEG)
    sel = jnp.zeros((MH, nb, TM), jnp.bool_)
    for _ in range(min(TOPK, nb)):
        gmax = jnp.max(g3, axis=1, keepdims=True)
        idx = jnp.min(jnp.where(g3 == gmax, blk, nb), axis=1, keepdims=True)
        pick = blk == idx
        sel = jnp.logical_or(sel, pick)
        g3 = jnp.where(pick, LOWEST, g3)
    bias_ref[0] = jnp.where(jnp.logical_or(jnp.logical_and(sel, past), blk == own), 0.0, NEG)

    s = jnp.dot(h, ws_ref[...], preferred_element_type=F32)
    lane = lax.broadcasted_iota(jnp.int32, s.shape, 1)
    beta = jax.nn.sigmoid(s)
    z = s + dtb_ref[...]
    softplus = jnp.maximum(z, 0.0) + jnp.log1p(jnp.exp(-jnp.abs(z)))
    g = -jnp.exp(alog_ref[...]) * softplus
    gb = jnp.where(lane < DH, beta, jnp.where(lane < 2 * DH, g, 0.0))
    gb_ref[0] = gb
    gbt = gb.T[0:8, :]
    for cidx in range(TM // CH):
        gbt_ref[0, cidx] = gbt[:, cidx * CH:(cidx + 1) * CH]

    sz_ref[0] = _silu(jnp.dot(h, wz_ref[...], preferred_element_type=F32))
    dx_ref[0] = jnp.dot(h, wd_ref[...], preferred_element_type=F32)


def _inproj(x, mod3, pre_w, w_in, a_log, dt_bias):
    B, S, _ = x.shape
    nb = S // BLK
    nt = S // TM
    wb = w_in.astype(BF16)
    wqk = wb[:, 0:2 * MW]
    wvt = wb[:, 2 * MW:3 * MW].T
    wg = wb[:, 3 * MW:4 * MW]
    wd = wb[:, 4 * MW:4 * MW + 3 * DW]
    wz = wb[:, 4 * MW + 3 * DW:4 * MW + 4 * DW]
    ws = jnp.pad(wb[:, 4 * MW + 4 * DW:], ((0, 0), (0, LANES - 2 * DH)))
    alog = jnp.pad(a_log.astype(F32), (DH, LANES - 2 * DH)).reshape(1, LANES)
    dtb = jnp.pad(dt_bias.astype(F32), (DH, LANES - 2 * DH)).reshape(1, LANES)

    half = ROPE_D // 2
    inv_freq = jnp.power(ROPE_THETA, -jnp.arange(0, ROPE_D, 2, dtype=F32) / ROPE_D)
    ang = jnp.arange(S, dtype=jnp.int32).astype(F32)[:, None] * inv_freq[None, :]
    cos, sin = jnp.cos(ang), jnp.sin(ang)
    one = jnp.ones((S, MD - ROPE_D), F32)
    zero = jnp.zeros((S, MD - ROPE_D), F32)
    zh = jnp.zeros((S, half), F32)
    rc = jnp.tile(jnp.concatenate([cos, cos, one], axis=1), (1, 2))
    rs1 = jnp.tile(jnp.concatenate([-sin, zh, zero], axis=1), (1, 2))
    rs2 = jnp.tile(jnp.concatenate([zh, sin, zero], axis=1), (1, 2))

    const = lambda shape: pl.BlockSpec(shape, lambda b, t: (0,) * len(shape))
    row = lambda w: pl.BlockSpec((1, TM, w), lambda b, t: (b, t, 0))
    tab = pl.BlockSpec((TM, LANES), lambda b, t: (t, 0))
    out_shapes = [
        jax.ShapeDtypeStruct((B, S, MW), BF16),
        jax.ShapeDtypeStruct((B, S, MW), BF16),
        jax.ShapeDtypeStruct((B, MH, nb, S), F32),
        jax.ShapeDtypeStruct((B, nb, MW, BLK), BF16),
        jax.ShapeDtypeStruct((B, S, MW), F32),
        jax.ShapeDtypeStruct((B, S, 3 * DW), F32),
        jax.ShapeDtypeStruct((B, S, DW), F32),
        jax.ShapeDtypeStruct((B, S, LANES), F32),
        jax.ShapeDtypeStruct((B, S // CH, 8, CH), F32),
    ]
    out_specs = [
        row(MW), row(MW),
        pl.BlockSpec((1, MH, nb, TM), lambda b, t: (b, 0, 0, t)),
        pl.BlockSpec((1, NBT, MW, BLK), lambda b, t: (b, t, 0, 0)),
        row(MW), row(3 * DW), row(DW), row(LANES),
        pl.BlockSpec((1, TM // CH, 8, CH), lambda b, t: (b, t, 0, 0)),
    ]
    return pl.pallas_call(
        functools.partial(_inproj_kernel, nb=nb),
        grid=(B, nt),
        in_specs=[row(D), pl.BlockSpec((1, 3, D), lambda b, t: (b, 0, 0)), const((1, D)),
                  const((D, 2 * MW)), const((MW, D)), const((D, MW)), const((D, 3 * DW)),
                  const((D, DW)), const((D, LANES)), tab, tab, tab,
                  const((1, LANES)), const((1, LANES))],
        out_specs=out_specs,
        out_shape=out_shapes,
        scratch_shapes=[pltpu.VMEM((nb, MW), F32)],
        compiler_params=pltpu.CompilerParams(
            dimension_semantics=("parallel", "arbitrary"), vmem_limit_bytes=VMEM_LIMIT),
        name="inproj",
    )(x, mod3, pre_w.reshape(1, D), wqk, wvt, wg, wd, wz, ws, rc, rs1, rs2, alog, dtb)


NPAIR = MW // LANES


def _moba_kernel(q_ref, k_ref, vt_ref, bias_ref, sg_ref, o_ref,
                 pm_ref, s0_ref, s1_ref, qm_ref, acc_ref):
    i = pl.program_id(1)
    lane = lax.broadcasted_iota(jnp.int32, (1, LANES), 1)
    head_lanes = [lane < MD, lane >= MD]

    def fold8(a):
        return a.reshape(BLK // 8, 8, BLK)

    @pl.when(i == 0)
    def _():
        kpos = lax.broadcasted_iota(jnp.int32, (BLK, BLK), 0)
        qpos = lax.broadcasted_iota(jnp.int32, (BLK, BLK), 1)
        pm_ref[0] = jnp.zeros((BLK, BLK), F32)
        pm_ref[1] = jnp.where(kpos <= qpos, 0.0, NEG)
        pm_ref[2] = jnp.full((BLK, BLK), NEG, F32)

    ngroups = i // PVG + 1

    s_refs = (s0_ref, s1_ref)

    def scores(pr, qs, j, mx):
        kj = k_ref[0, pl.ds(pl.multiple_of(j * BLK, BLK), BLK), pr * LANES:(pr + 1) * LANES]
        pm = pm_ref[jnp.clip(j - i, -1, 1) + 1]
        out = []
        for h in range(2):
            s = lax.dot_general(kj, qs[h][...], NT, preferred_element_type=F32) + pm
            s_refs[pr % 2][h, j] = s
            out.append(jnp.maximum(mx[h], jnp.max(fold8(s), axis=0) + bias_ref[0, 2 * pr + h, pl.ds(j, 1), :]))
        return tuple(out)

    def probs_pv(pr, m, j, ls):
        out = []
        for h in range(2):
            p = jnp.exp2(s_refs[pr % 2][h, j] - (m[h] - bias_ref[0, 2 * pr + h, pl.ds(j, 1), :]))
            acc_ref[h] += jnp.dot(vt_ref[0, j, pr * LANES + h * MD:pr * LANES + (h + 1) * MD, :], p.astype(BF16),
                                  preferred_element_type=F32)
            out.append(ls[h] + jnp.sum(fold8(p), axis=0))
        return tuple(out)

    m = {}
    for stage in range(NPAIR + 1):
        p1 = stage if stage < NPAIR else None
        p2 = stage - 1 if stage >= 1 else None
        qs = None
        if p1 is not None:
            qp = q_ref[0, :, p1 * LANES:(p1 + 1) * LANES]
            for h in range(2):
                qm_ref[p1 % 2, h] = jnp.where(head_lanes[h], qp, jnp.zeros_like(qp))
            qs = [qm_ref.at[p1 % 2, h] for h in range(2)]
        if p2 is not None:
            acc_ref[...] = jnp.zeros_like(acc_ref)

        def body(t, carry):
            mx, ls = carry
            for dj in range(PVG):
                if p1 is not None:
                    mx = scores(p1, qs, t * PVG + dj, mx)
                if p2 is not None:
                    ls = probs_pv(p2, m[p2], t * PVG + dj, ls)
            return mx, ls

        init = (tuple(jnp.full((8, BLK), NEG, F32) for _ in range(2)) if p1 is not None else (),
                tuple(jnp.zeros((8, BLK), F32) for _ in range(2)) if p2 is not None else ())
        mx, ls = lax.fori_loop(0, ngroups, body, init)
        if p1 is not None:
            m[p1] = [jnp.max(mx[h], axis=0, keepdims=True) for h in range(2)]
        if p2 is not None:
            ot = jnp.concatenate([acc_ref[h] / jnp.sum(ls[h], axis=0, keepdims=True) for h in range(2)],
                                 axis=0)
            sl = slice(p2 * LANES, (p2 + 1) * LANES)
            o_ref[0, :, sl] = (ot.T * sg_ref[0, :, sl]).astype(o_ref.dtype)


def _moba(q, k, vt, bias, sg):
    B, S, _ = q.shape
    nb = S // BLK
    return pl.pallas_call(
        _moba_kernel,
        grid=(B, nb),
        in_specs=[pl.BlockSpec((1, BLK, MW), lambda b, i: (b, i, 0)),
                  pl.BlockSpec((1, S, MW), lambda b, i: (b, 0, 0)),
                  pl.BlockSpec((1, nb, MW, BLK), lambda b, i: (b, 0, 0, 0)),
                  pl.BlockSpec((1, MH, nb, BLK), lambda b, i: (b, 0, 0, i)),
                  pl.BlockSpec((1, BLK, MW), lambda b, i: (b, i, 0))],
        out_specs=pl.BlockSpec((1, BLK, MW), lambda b, i: (b, i, 0)),
        out_shape=jax.ShapeDtypeStruct((B, S, MW), BF16),
        scratch_shapes=[pltpu.VMEM((3, BLK, BLK), F32),
                        pltpu.VMEM((2, nb, BLK, BLK), F32),
                        pltpu.VMEM((2, nb, BLK, BLK), F32),
                        pltpu.VMEM((2, 2, BLK, LANES), BF16),
                        pltpu.VMEM((2, MD, BLK), F32)],
        compiler_params=pltpu.CompilerParams(
            dimension_semantics=("parallel", "arbitrary"), vmem_limit_bytes=VMEM_LIMIT),
        name="moba",
    )(q, k, vt, bias, sg)


def _dn_kernel(x_ref, sz_ref, gb_ref, gbt_ref, cw_ref, nw_ref, o_ref, xbuf, qkv, state, mqs, cs, os_, gls):
    t = pl.program_id(1)
    halo = 8

    @pl.when(t == 0)
    def _():
        xbuf[0:halo, :] = jnp.zeros((halo, 3 * DW), F32)
        state[...] = jnp.zeros_like(state)

    @pl.when(t > 0)
    def _():
        xbuf[0:halo, :] = xbuf[TD:TD + halo, :]

    xbuf[halo:halo + TD, :] = x_ref[0]

    for n in range(3 * DH):
        sl = slice(n * DK, (n + 1) * DK)
        y = jnp.zeros((TD, DK), F32)
        for tap in range(CONV):
            lo = halo - (CONV - 1) + tap
            y = y + cw_ref[tap:tap + 1, sl] * xbuf[lo:lo + TD, sl]
        y = _silu(y)
        if n < 2 * DH:
            y = y * lax.rsqrt(jnp.sum(y * y, axis=-1, keepdims=True) + EPS)
        if n < DH:
            y = y * (DK ** -0.5)
        qkv[:, sl] = y

    r = lax.broadcasted_iota(jnp.int32, (CH, CH), 0)
    c = lax.broadcasted_iota(jnp.int32, (CH, CH), 1)
    tril = r >= c
    r2 = lax.broadcasted_iota(jnp.int32, (CH, 2 * CH), 0)
    c2 = lax.broadcasted_iota(jnp.int32, (CH, 2 * CH), 1)
    c4 = lax.broadcasted_iota(jnp.int32, (CH, 2 * DK), 1)
    keep = jnp.logical_or(jnp.logical_and(c2 < CH, c2 >= r2), c2 - CH > r2)
    ones_l3 = jnp.concatenate([tril.astype(F32)] * 3, axis=1).astype(BF16)
    ones_u3 = jnp.concatenate([(r <= c).astype(F32)] * 3, axis=0).astype(BF16)
    nw = nw_ref[...]

    def local(it, _):
        units = [(dc, h) for dc in range(CU) for h in range(DH)]
        ci = [it * CU + dc for dc in range(CU)]
        r0 = [pl.multiple_of(c_ * CH, CH) for c_ in ci]
        gcol = [gb_ref[0, pl.ds(r0[dc], CH), :] for dc in range(CU)]
        grow = [gbt_ref[0, ci[dc]] for dc in range(CU)]
        gc = [jnp.dot(ones_l3, jnp.concatenate(_split3_bf16(g), axis=0).astype(BF16),
                      preferred_element_type=F32) for g in gcol]
        gr = [jnp.dot(jnp.concatenate(_split3_bf16(g), axis=1).astype(BF16), ones_u3,
                      preferred_element_type=F32) for g in grow]
        q = [qkv[pl.ds(r0[dc], CH), h * DK:(h + 1) * DK] for dc, h in units]
        k = [qkv[pl.ds(r0[dc], CH), DW + h * DK:DW + (h + 1) * DK] for dc, h in units]
        v = [qkv[pl.ds(r0[dc], CH), 2 * DW + h * DK:2 * DW + (h + 1) * DK] for dc, h in units]
        beta = [gcol[dc][:, h:h + 1] for dc, h in units]
        gcum = [gc[dc][:, DH + h:DH + h + 1] for dc, h in units]
        glast = [g[CH - 1:CH, :] for g in gcum]
        us = range(len(units))
        eg = [jnp.exp(g) for g in gcum]
        kb = [k[u] * beta[u] for u in us]
        dT = []
        for u, (dc, h) in enumerate(units):
            g2 = jnp.concatenate([gr[dc][DH + h:DH + h + 1, :]] * 2, axis=1)
            dT.append(jnp.where(keep, jnp.exp(jnp.where(keep, g2 - gcum[u], 0.0)), 0.0))
        aqt = [lax.dot_general(k[u].astype(BF16), jnp.concatenate([q[u], kb[u]], axis=0).astype(BF16), NT,
                               preferred_element_type=F32) for u in us]
        w = [jnp.concatenate([k[u] * jnp.exp(glast[u] - gcum[u]), aqt[u] * dT[u]], axis=1) for u in us]
        for lvl in range(6):
            for u in us:
                wh, wlo = _split_bf16(w[u])
                ph = pltpu.roll(wh[:, DK:], CH, axis=1)
                lhs = jnp.concatenate([jnp.where(c2 < CH, ph, wlo[:, DK:]), ph[:, :CH]], axis=1)
                res = jnp.dot(lhs.astype(BF16), jnp.concatenate([wh, wh, wlo], axis=0).astype(BF16),
                              preferred_element_type=F32)
                w[u] = jnp.where(c4 < 3 * CH, w[u] - res if lvl == 0 else w[u] + res, res)
        x0 = [jnp.concatenate([v[u] * beta[u], kb[u] * eg[u]], axis=1).astype(BF16) for u in us]
        rr = [jnp.dot(w[u].T.astype(BF16), x0[u], preferred_element_type=F32) for u in us]
        for dc in range(CU):
            gls[ci[dc]] = jnp.exp(gc[dc][CH - 1:CH, :])
        for u, (dc, h) in enumerate(units):
            mqs[h, ci[dc], 0:DK, :] = rr[u][0:DK, DK:].astype(BF16)
            mqs[h, ci[dc], DK:DK + CH, :] = (q[u] * eg[u] - rr[u][DK:DK + CH, DK:]).astype(BF16)
            cs[h, ci[dc]] = rr[u][0:DK, :DK]
            os_[pl.ds(r0[dc], CH), h * DK:(h + 1) * DK] = rr[u][DK:DK + CH, :DK]
        return 0

    lax.fori_loop(0, TD // (CH * CU), local, 0)

    for ci in range(TD // CH):
        r0 = ci * CH
        gl_all = gls[ci]
        for h in range(DH):
            st = state[h]
            rr = jnp.dot(mqs[h, ci], st.astype(BF16), preferred_element_type=F32)
            state[h] = st * gl_all[:, DH + h:DH + h + 1] - rr[:DK] + cs[h, ci]
            o = rr[DK:] + os_[r0:r0 + CH, h * DK:(h + 1) * DK]
            o = o * lax.rsqrt(jnp.mean(o * o, axis=-1, keepdims=True) + EPS) * nw
            o = o * sz_ref[0, r0:r0 + CH, h * DK:(h + 1) * DK]
            o_ref[0, r0:r0 + CH, h * DK:(h + 1) * DK] = o.astype(o_ref.dtype)


def _deltanet(dx, sz, gb, gbt, conv_w, dn_norm_w):
    B, S, _ = dx.shape
    nt = S // TD
    return pl.pallas_call(
        _dn_kernel,
        grid=(B, nt),
        in_specs=[pl.BlockSpec((1, TD, 3 * DW), lambda b, t: (b, t, 0)),
                  pl.BlockSpec((1, TD, DW), lambda b, t: (b, t, 0)),
                  pl.BlockSpec((1, TD, LANES), lambda b, t: (b, t, 0)),
                  pl.BlockSpec((1, TD // CH, 8, CH), lambda b, t: (b, t, 0, 0)),
                  pl.BlockSpec((CONV, 3 * DW), lambda b, t: (0, 0)),
                  pl.BlockSpec((1, DK), lambda b, t: (0, 0))],
        out_specs=pl.BlockSpec((1, TD, DW), lambda b, t: (b, t, 0)),
        out_shape=jax.ShapeDtypeStruct((B, S, DW), BF16),
        scratch_shapes=[pltpu.VMEM((TD + 8, 3 * DW), F32),
                        pltpu.VMEM((TD, 3 * DW), F32),
                        pltpu.VMEM((DH, DK, DK), F32),
                        pltpu.VMEM((DH, TD // CH, DK + CH, DK), BF16),
                        pltpu.VMEM((DH, TD // CH, DK, DK), F32),
                        pltpu.VMEM((TD, DW), F32),
                        pltpu.VMEM((TD // CH, 1, LANES), F32)],
        compiler_params=pltpu.CompilerParams(
            dimension_semantics=("parallel", "arbitrary"), vmem_limit_bytes=VMEM_LIMIT),
        name="deltanet",
    )(dx, sz, gb, gbt, conv_w, dn_norm_w.reshape(1, DK))


def _outproj_kernel(om_ref, od_ref, x_ref, mod_ref, pw_ref, wt_ref, wb_ref, o_ref):
    y = jnp.dot(om_ref[0], wt_ref[...], preferred_element_type=F32)
    y = y + jnp.dot(od_ref[0], wb_ref[...], preferred_element_type=F32)
    yn = y * lax.rsqrt(jnp.mean(y * y, axis=-1, keepdims=True) + EPS) * pw_ref[...]
    o_ref[0] = x_ref[0] + mod_ref[0, 2:3, :] * yn


def _outproj(om, od, x, mod3, post_w, w_out):
    B, S, _ = x.shape
    wb = w_out.astype(BF16)
    row = lambda w: pl.BlockSpec((1, TO, w), lambda b, t: (b, t, 0))
    return pl.pallas_call(
        _outproj_kernel,
        grid=(B, S // TO),
        in_specs=[row(MW), row(DW), row(D),
                  pl.BlockSpec((1, 3, D), lambda b, t: (b, 0, 0)),
                  pl.BlockSpec((1, D), lambda b, t: (0, 0)),
                  pl.BlockSpec((MW, D), lambda b, t: (0, 0)),
                  pl.BlockSpec((DW, D), lambda b, t: (0, 0))],
        out_specs=row(D),
        out_shape=jax.ShapeDtypeStruct((B, S, D), F32),
        compiler_params=pltpu.CompilerParams(
            dimension_semantics=("parallel", "parallel"), vmem_limit_bytes=VMEM_LIMIT),
        name="outproj",
    )(om, od, x, mod3, post_w.reshape(1, D), wb[:MW], wb[MW:])


def kernel(x, c, ada_w, ada_b, pre_norm_w, post_norm_w, w_in, conv_w, a_log, dt_bias, dn_norm_w, w_out):
    B, S, _ = x.shape
    assert S % TD == 0 and S % TO == 0 and S % (BLK * PVG) == 0 and S % TM == 0 and TM % BLK == 0
    depth = ada_w.shape[0]
    for l in range(depth):
        mod3 = _mod(c, ada_w[l], ada_b[l]).reshape(B, 3, D)
        q, k, bias, vt, sg, dx, sz, gb, gbt = _inproj(x, mod3, pre_norm_w[l], w_in[l], a_log[l], dt_bias[l])
        om = _moba(q, k, vt, bias, sg)
        od = _deltanet(dx, sz, gb, gbt, conv_w[l], dn_norm_w[l])
        x = _outproj(om, od, x, mod3, post_norm_w[l], w_out[l])
    return x
```

```python
import functools
import math

import jax
import jax.numpy as jnp
from jax import lax
from jax.experimental import pallas as pl
from jax.experimental.pallas import tpu as pltpu

F32 = jnp.float32
BF16 = jnp.bfloat16
HI = lax.Precision.HIGHEST

D = 1024
MH, MD = 8, 64
MW = MH * MD
BLK = 256
TOPK = 3
ROPE_D = 16
ROPE_THETA = 500000.0
DH, DK = 4, 128
DW = DH * DK
CONV = 4
CH = 64
EPS = 1e-6
NEG = -1e30
LOWEST = -3.0e38
LOG2E = math.log2(math.e)
PVG = 4

LANES = 128
VMEM_LIMIT = 56 * 1024 * 1024

TM = 512
NBT = TM // BLK
TD = 512
CU = 8
TO = 1024

NT = (((1,), (1,)), ((), ()))
TN = (((0,), (0,)), ((), ()))


def _silu(v):
    h = 0.5 * v
    return h + h * jnp.tanh(h)


def _split_bf16(p):
    hi = p.astype(BF16).astype(F32)
    lo = (p - hi).astype(BF16).astype(F32)
    return hi, lo


def _split3_bf16(p):
    a, r = _split_bf16(p)
    c = (p - a - r).astype(BF16).astype(F32)
    return a, r, c


def _mod_kernel(c_ref, w_ref, b_ref, o_ref):
    c = c_ref[...]
    o_ref[...] = jnp.dot(_silu(c), w_ref[...], precision=HI, preferred_element_type=F32) + b_ref[...]


def _mod(c, ada_w, ada_b):
    B = c.shape[0]
    tn = 512
    return pl.pallas_call(
        _mod_kernel,
        grid=(3 * D // tn,),
        in_specs=[pl.BlockSpec((B, D), lambda j: (0, 0)),
                  pl.BlockSpec((D, tn), lambda j: (0, j)),
                  pl.BlockSpec((1, tn), lambda j: (0, j))],
        out_specs=pl.BlockSpec((B, tn), lambda j: (0, j)),
        out_shape=jax.ShapeDtypeStruct((B, 3 * D), F32),
        name="mod",
    )(c, ada_w, ada_b.reshape(1, 3 * D))


def _inproj_kernel(x_ref, mod_ref, pw_ref, wqk_ref, wvt_ref, wg_ref, wd_ref, wz_ref, ws_ref,
                   rc_ref, rs1_ref, rs2_ref, alog_ref, dtb_ref,
                   q_ref, k_ref, bias_ref, vt_ref, sg_ref, dx_ref, sz_ref, gb_ref, gbt_ref, kms, *, nb):
    t = pl.program_id(1)
    x = x_ref[0]
    shift = mod_ref[0, 0:1, :]
    scale = mod_ref[0, 1:2, :]
    y = x * lax.rsqrt(jnp.mean(x * x, axis=-1, keepdims=True) + EPS) * pw_ref[...]
    h = (y * (1.0 + scale) + shift).astype(BF16)

    rc, rs1, rs2 = rc_ref[...], rs1_ref[...], rs2_ref[...]

    def rope(t):
        tm1 = pltpu.roll(t, MW - ROPE_D // 2, axis=1)
        tp1 = pltpu.roll(t, ROPE_D // 2, axis=1)
        parts = []
        for n in range(MW // LANES):
            sl = slice(n * LANES, (n + 1) * LANES)
            parts.append(t[:, sl] * rc + tm1[:, sl] * rs1 + tp1[:, sl] * rs2)
        return jnp.concatenate(parts, axis=1)

    qk = jnp.dot(h, wqk_ref[...], preferred_element_type=F32)
    q = rope(qk[:, :MW])
    k = rope(qk[:, MW:])
    k_ref[0] = k.astype(BF16)
    q_ref[0] = (q * (MD ** -0.5 * LOG2E)).astype(BF16)

    vt = lax.dot_general(wvt_ref[...], h, NT, preferred_element_type=F32)
    for j in range(NBT):
        vt_ref[0, j] = vt[:, j * BLK:(j + 1) * BLK].astype(BF16)

    sg_ref[0] = _silu(jnp.dot(h, wg_ref[...], preferred_element_type=F32))

    @pl.when(t == 0)
    def _():
        kms[...] = jnp.zeros_like(kms)

    for j in range(NBT):
        kms[pl.ds(t * NBT + j, 1), :] = jnp.mean(k[j * BLK:(j + 1) * BLK], axis=0, keepdims=True)
    km = kms[...]
    wl = lax.broadcasted_iota(jnp.int32, (1, MW), 1)
    kmh = jnp.concatenate([jnp.where((wl >= hh * MD) & (wl < (hh + 1) * MD), km, 0.0) for hh in range(MH)], axis=0)
    kh, kl = _split_bf16(kmh)
    qh, ql = _split_bf16(q)
    gate = lax.dot_general(jnp.concatenate([kh, kl, kh], axis=1).astype(BF16),
                           jnp.concatenate([qh, qh, ql], axis=1).astype(BF16), NT,
                           preferred_element_type=F32)
    blk = lax.broadcasted_iota(jnp.int32, (MH, nb, TM), 1)
    qpos = lax.broadcasted_iota(jnp.int32, (MH, nb, TM), 2)
    own = t * NBT + sum((qpos >= j * BLK).astype(jnp.int32) for j in range(1, NBT))
    past = blk < own
    g3 = jnp.where(past, gate.reshape(MH, nb, TM), NEG)
    sel = jnp.zeros((MH, nb, TM), jnp.bool_)
    for _ in range(min(TOPK, nb)):
        gmax = jnp.max(g3, axis=1, keepdims=True)
        idx = jnp.min(jnp.where(g3 == gmax, blk, nb), axis=1, keepdims=True)
        pick = blk == idx
        sel = jnp.logical_or(sel, pick)
        g3 = jnp.where(pick, LOWEST, g3)
    bias_ref[0] = jnp.where(jnp.logical_or(jnp.logical_and(sel, past), blk == own), 0.0, NEG)

    s = jnp.dot(h, ws_ref[...], preferred_element_type=F32)
    lane = lax.broadcasted_iota(jnp.int32, s.shape, 1)
    beta = jax.nn.sigmoid(s)
    z = s + dtb_ref[...]
    softplus = jnp.maximum(z, 0.0) + jnp.log1p(jnp.exp(-jnp.abs(z)))
    g = -jnp.exp(alog_ref[...]) * softplus
    gb = jnp.where(lane < DH, beta, jnp.where(lane < 2 * DH, g, 0.0))
    gb_ref[0] = gb
    gbt = gb.T[0:8, :]
    for cidx in range(TM // CH):
        gbt_ref[0, cidx] = gbt[:, cidx * CH:(cidx + 1) * CH]

    sz_ref[0] = _silu(jnp.dot(h, wz_ref[...], preferred_element_type=F32))
    dx_ref[0] = jnp.dot(h, wd_ref[...], preferred_element_type=F32)


def _inproj(x, mod3, pre_w, w_in, a_log, dt_bias):
    B, S, _ = x.shape
    nb = S // BLK
    nt = S // TM
    wb = w_in.astype(BF16)
    wqk = wb[:, 0:2 * MW]
    wvt = wb[:, 2 * MW:3 * MW].T
    wg = wb[:, 3 * MW:4 * MW]
    wd = wb[:, 4 * MW:4 * MW + 3 * DW]
    wz = wb[:, 4 * MW + 3 * DW:4 * MW + 4 * DW]
    ws = jnp.pad(wb[:, 4 * MW + 4 * DW:], ((0, 0), (0, LANES - 2 * DH)))
    alog = jnp.pad(a_log.astype(F32), (DH, LANES - 2 * DH)).reshape(1, LANES)
    dtb = jnp.pad(dt_bias.astype(F32), (DH, LANES - 2 * DH)).reshape(1, LANES)

    half = ROPE_D // 2
    inv_freq = jnp.power(ROPE_THETA, -jnp.arange(0, ROPE_D, 2, dtype=F32) / ROPE_D)
    ang = jnp.arange(S, dtype=jnp.int32).astype(F32)[:, None] * inv_freq[None, :]
    cos, sin = jnp.cos(ang), jnp.sin(ang)
    one = jnp.ones((S, MD - ROPE_D), F32)
    zero = jnp.zeros((S, MD - ROPE_D), F32)
    zh = jnp.zeros((S, half), F32)
    rc = jnp.tile(jnp.concatenate([cos, cos, one], axis=1), (1, 2))
    rs1 = jnp.tile(jnp.concatenate([-sin, zh, zero], axis=1), (1, 2))
    rs2 = jnp.tile(jnp.concatenate([zh, sin, zero], axis=1), (1, 2))

    const = lambda shape: pl.BlockSpec(shape, lambda b, t: (0,) * len(shape))
    row = lambda w: pl.BlockSpec((1, TM, w), lambda b, t: (b, t, 0))
    tab = pl.BlockSpec((TM, LANES), lambda b, t: (t, 0))
    out_shapes = [
        jax.ShapeDtypeStruct((B, S, MW), BF16),
        jax.ShapeDtypeStruct((B, S, MW), BF16),
        jax.ShapeDtypeStruct((B, MH, nb, S), F32),
        jax.ShapeDtypeStruct((B, nb, MW, BLK), BF16),
        jax.ShapeDtypeStruct((B, S, MW), F32),
        jax.ShapeDtypeStruct((B, S, 3 * DW), F32),
        jax.ShapeDtypeStruct((B, S, DW), F32),
        jax.ShapeDtypeStruct((B, S, LANES), F32),
        jax.ShapeDtypeStruct((B, S // CH, 8, CH), F32),
    ]
    out_specs = [
        row(MW), row(MW),
        pl.BlockSpec((1, MH, nb, TM), lambda b, t: (b, 0, 0, t)),
        pl.BlockSpec((1, NBT, MW, BLK), lambda b, t: (b, t, 0, 0)),
        row(MW), row(3 * DW), row(DW), row(LANES),
        pl.BlockSpec((1, TM // CH, 8, CH), lambda b, t: (b, t, 0, 0)),
    ]
    return pl.pallas_call(
        functools.partial(_inproj_kernel, nb=nb),
        grid=(B, nt),
        in_specs=[row(D), pl.BlockSpec((1, 3, D), lambda b, t: (b, 0, 0)), const((1, D)),
                  const((D, 2 * MW)), const((MW, D)), const((D, MW)), const((D, 3 * DW)),
                  const((D, DW)), const((D, LANES)), tab, tab, tab,
                  const((1, LANES)), const((1, LANES))],
        out_specs=out_specs,
        out_shape=out_shapes,
        scratch_shapes=[pltpu.VMEM((nb, MW), F32)],
        compiler_params=pltpu.CompilerParams(
            dimension_semantics=("parallel", "arbitrary"), vmem_limit_bytes=VMEM_LIMIT),
        name="inproj",
    )(x, mod3, pre_w.reshape(1, D), wqk, wvt, wg, wd, wz, ws, rc, rs1, rs2, alog, dtb)


NPAIR = MW // LANES
ONES = 16


def _moba_kernel(q_ref, k_ref, vt_ref, bias_ref, sg_ref, qn_ref, biasn_ref, o_ref,
                 pm_ref, s0_ref, s1_ref, qm_ref, acc_ref, m0_ref):
    i = pl.program_id(1)
    lane = lax.broadcasted_iota(jnp.int32, (1, LANES), 1)
    head_lanes = [lane < MD, lane >= MD]

    def fold8(a):
        return a.reshape(BLK // 8, 8, BLK)

    @pl.when(i == 0)
    def _():
        kpos = lax.broadcasted_iota(jnp.int32, (BLK, BLK), 0)
        qpos = lax.broadcasted_iota(jnp.int32, (BLK, BLK), 1)
        pm_ref[0] = jnp.zeros((BLK, BLK), F32)
        pm_ref[1] = jnp.where(kpos <= qpos, 0.0, NEG)
        pm_ref[2] = jnp.full((BLK, BLK), NEG, F32)

    ngroups = i // PVG + 1

    s_refs = (s0_ref, s1_ref)

    nb = bias_ref.shape[2]
    ones_rows = jnp.ones((ONES, BLK), BF16)

    def mask_q(slot, qp):
        for h in range(2):
            qm_ref[slot, h] = jnp.where(head_lanes[h], qp, jnp.zeros_like(qp))
        return [qm_ref.at[slot, h] for h in range(2)]

    def scores(pr, qs, j, mx, tile, bias, hoff):
        kj = k_ref[0, pl.ds(pl.multiple_of(j * BLK, BLK), BLK), pr * LANES:(pr + 1) * LANES]
        pm = pm_ref[jnp.clip(j - tile, -1, 1) + 1]
        out = []
        for h in range(2):
            s = lax.dot_general(kj, qs[h][...], NT, preferred_element_type=F32) + pm
            s_refs[pr % 2][h, j] = s
            out.append(jnp.maximum(mx[h], jnp.max(fold8(s), axis=0) + bias[0, hoff + h, pl.ds(j, 1), :]))
        return tuple(out)

    def probs_pv(pr, m, j):
        for h in range(2):
            p = jnp.exp2(s_refs[pr % 2][h, j] - (m[h] - bias_ref[0, 2 * pr + h, pl.ds(j, 1), :]))
            vth = vt_ref[0, j, pr * LANES + h * MD:pr * LANES + (h + 1) * MD, :]
            acc_ref[h] += jnp.dot(jnp.concatenate([vth, ones_rows], axis=0), p.astype(BF16),
                                  preferred_element_type=F32)

    def finalize(pr):
        ot = jnp.concatenate([acc_ref[h, 0:MD, :] / acc_ref[h, MD:MD + 1, :] for h in range(2)], axis=0)
        sl = slice(pr * LANES, (pr + 1) * LANES)
        o_ref[0, :, sl] = (ot.T * sg_ref[0, :, sl]).astype(o_ref.dtype)

    def mx_init():
        return tuple(jnp.full((8, BLK), NEG, F32) for _ in range(2))

    def colmax(mx):
        return [jnp.max(mx[h], axis=0, keepdims=True) for h in range(2)]

    @pl.when(i == 0)
    def _():
        qs = mask_q(0, q_ref[0, :, 0:LANES])

        def body(t, mx):
            for dj in range(PVG):
                mx = scores(0, qs, t * PVG + dj, mx, i, bias_ref, 0)
            return mx

        cm = colmax(lax.fori_loop(0, ngroups, body, mx_init()))
        for h in range(2):
            m0_ref[h] = cm[h]

    m = {0: [m0_ref[h] for h in range(2)]}
    for stage in range(1, NPAIR):
        p1, p2 = stage, stage - 1
        qs = mask_q(p1 % 2, q_ref[0, :, p1 * LANES:(p1 + 1) * LANES])
        acc_ref[...] = jnp.zeros_like(acc_ref)

        def body(t, mx):
            for dj in range(PVG):
                mx = scores(p1, qs, t * PVG + dj, mx, i, bias_ref, 2 * p1)
                probs_pv(p2, m[p2], t * PVG + dj)
            return mx

        m[p1] = colmax(lax.fori_loop(0, ngroups, body, mx_init()))
        finalize(p2)

    last = NPAIR - 1
    acc_ref[...] = jnp.zeros_like(acc_ref)

    @pl.when(i < nb - 1)
    def _():
        qs = mask_q(0, qn_ref[0])

        def body(t, mx):
            for dj in range(PVG):
                mx = scores(0, qs, t * PVG + dj, mx, i + 1, biasn_ref, 0)
                probs_pv(last, m[last], t * PVG + dj)
            return mx

        def extra(t, mx):
            for dj in range(PVG):
                mx = scores(0, qs, t * PVG + dj, mx, i + 1, biasn_ref, 0)
            return mx

        mx = lax.fori_loop(0, ngroups, body, mx_init())
        cm = colmax(lax.fori_loop(ngroups, (i + 1) // PVG + 1, extra, mx))
        for h in range(2):
            m0_ref[h] = cm[h]

    @pl.when(i == nb - 1)
    def _():
        def body(t, carry):
            for dj in range(PVG):
                probs_pv(last, m[last], t * PVG + dj)
            return carry

        lax.fori_loop(0, ngroups, body, 0)

    finalize(last)


def _moba(q, k, vt, bias, sg):
    B, S, _ = q.shape
    nb = S // BLK
    return pl.pallas_call(
        _moba_kernel,
        grid=(B, nb),
        in_specs=[pl.BlockSpec((1, BLK, MW), lambda b, i: (b, i, 0)),
                  pl.BlockSpec((1, S, MW), lambda b, i: (b, 0, 0)),
                  pl.BlockSpec((1, nb, MW, BLK), lambda b, i: (b, 0, 0, 0)),
                  pl.BlockSpec((1, MH, nb, BLK), lambda b, i: (b, 0, 0, i)),
                  pl.BlockSpec((1, BLK, MW), lambda b, i: (b, i, 0)),
                  pl.BlockSpec((1, BLK, LANES), lambda b, i: (b, jnp.minimum(i + 1, nb - 1), 0)),
                  pl.BlockSpec((1, 2, nb, BLK), lambda b, i: (b, 0, 0, jnp.minimum(i + 1, nb - 1)))],
        out_specs=pl.BlockSpec((1, BLK, MW), lambda b, i: (b, i, 0)),
        out_shape=jax.ShapeDtypeStruct((B, S, MW), BF16),
        scratch_shapes=[pltpu.VMEM((3, BLK, BLK), F32),
                        pltpu.VMEM((2, nb, BLK, BLK), F32),
                        pltpu.VMEM((2, nb, BLK, BLK), F32),
                        pltpu.VMEM((2, 2, BLK, LANES), BF16),
                        pltpu.VMEM((2, MD + ONES, BLK), F32),
                        pltpu.VMEM((2, 1, BLK), F32)],
        compiler_params=pltpu.CompilerParams(
            dimension_semantics=("parallel", "arbitrary"), vmem_limit_bytes=VMEM_LIMIT),
        name="moba",
    )(q, k, vt, bias, sg, q, bias)


def _dn_kernel(x_ref, sz_ref, gb_ref, gbt_ref, cw_ref, nw_ref, o_ref, xbuf, qkv, state, mqs, cs, os_, gls):
    t = pl.program_id(1)
    halo = 8

    @pl.when(t == 0)
    def _():
        xbuf[0:halo, :] = jnp.zeros((halo, 3 * DW), F32)
        state[...] = jnp.zeros_like(state)

    @pl.when(t > 0)
    def _():
        xbuf[0:halo, :] = xbuf[TD:TD + halo, :]

    xbuf[halo:halo + TD, :] = x_ref[0]

    for n in range(3 * DH):
        sl = slice(n * DK, (n + 1) * DK)
        y = jnp.zeros((TD, DK), F32)
        for tap in range(CONV):
            lo = halo - (CONV - 1) + tap
            y = y + cw_ref[tap:tap + 1, sl] * xbuf[lo:lo + TD, sl]
        y = _silu(y)
        if n < 2 * DH:
            y = y * lax.rsqrt(jnp.sum(y * y, axis=-1, keepdims=True) + EPS)
        if n < DH:
            y = y * (DK ** -0.5)
        qkv[:, sl] = y

    r = lax.broadcasted_iota(jnp.int32, (CH, CH), 0)
    c = lax.broadcasted_iota(jnp.int32, (CH, CH), 1)
    tril = r >= c
    r2 = lax.broadcasted_iota(jnp.int32, (CH, 2 * CH), 0)
    c2 = lax.broadcasted_iota(jnp.int32, (CH, 2 * CH), 1)
    c4 = lax.broadcasted_iota(jnp.int32, (CH, 2 * DK), 1)
    keep = jnp.logical_or(jnp.logical_and(c2 < CH, c2 >= r2), c2 - CH > r2)
    ones_l3 = jnp.concatenate([tril.astype(F32)] * 3, axis=1).astype(BF16)
    ones_u3 = jnp.concatenate([(r <= c).astype(F32)] * 3, axis=0).astype(BF16)
    nw = nw_ref[...]

    def local(it, _):
        units = [(dc, h) for dc in range(CU) for h in range(DH)]
        ci = [it * CU + dc for dc in range(CU)]
        r0 = [pl.multiple_of(c_ * CH, CH) for c_ in ci]
        gcol = [gb_ref[0, pl.ds(r0[dc], CH), :] for dc in range(CU)]
        grow = [gbt_ref[0, ci[dc]] for dc in range(CU)]
        gc = [jnp.dot(ones_l3, jnp.concatenate(_split3_bf16(g), axis=0).astype(BF16),
                      preferred_element_type=F32) for g in gcol]
        gr = [jnp.dot(jnp.concatenate(_split3_bf16(g), axis=1).astype(BF16), ones_u3,
                      preferred_element_type=F32) for g in grow]
        q = [qkv[pl.ds(r0[dc], CH), h * DK:(h + 1) * DK] for dc, h in units]
        k = [qkv[pl.ds(r0[dc], CH), DW + h * DK:DW + (h + 1) * DK] for dc, h in units]
        v = [qkv[pl.ds(r0[dc], CH), 2 * DW + h * DK:2 * DW + (h + 1) * DK] for dc, h in units]
        beta = [gcol[dc][:, h:h + 1] for dc, h in units]
        gcum = [gc[dc][:, DH + h:DH + h + 1] for dc, h in units]
        glast = [g[CH - 1:CH, :] for g in gcum]
        us = range(len(units))
        eg = [jnp.exp(g) for g in gcum]
        kb = [k[u] * beta[u] for u in us]
        dT = []
        for u, (dc, h) in enumerate(units):
            g2 = jnp.concatenate([gr[dc][DH + h:DH + h + 1, :]] * 2, axis=1)
            dT.append(jnp.where(keep, jnp.exp(jnp.where(keep, g2 - gcum[u], 0.0)), 0.0))
        aqt = [lax.dot_general(k[u].astype(BF16), jnp.concatenate([q[u], kb[u]], axis=0).astype(BF16), NT,
                               preferred_element_type=F32) for u in us]
        w = [jnp.concatenate([k[u] * jnp.exp(glast[u] - gcum[u]), aqt[u] * dT[u]], axis=1) for u in us]
        for lvl in range(6):
            for u in us:
                wh, wlo = _split_bf16(w[u])
                ph = pltpu.roll(wh[:, DK:], CH, axis=1)
                lhs = jnp.concatenate([jnp.where(c2 < CH, ph, wlo[:, DK:]), ph[:, :CH]], axis=1)
                res = jnp.dot(lhs.astype(BF16), jnp.concatenate([wh, wh, wlo], axis=0).astype(BF16),
                              preferred_element_type=F32)
                w[u] = jnp.where(c4 < 3 * CH, w[u] - res if lvl == 0 else w[u] + res, res)
        x0 = [jnp.concatenate([v[u] * beta[u], kb[u] * eg[u]], axis=1).astype(BF16) for u in us]
        rr = [jnp.dot(w[u].T.astype(BF16), x0[u], preferred_element_type=F32) for u in us]
        for dc in range(CU):
            gls[ci[dc]] = jnp.exp(gc[dc][CH - 1:CH, :])
        for u, (dc, h) in enumerate(units):
            mqs[h, ci[dc], 0:DK, :] = rr[u][0:DK, DK:].astype(BF16)
            mqs[h, ci[dc], DK:DK + CH, :] = (q[u] * eg[u] - rr[u][DK:DK + CH, DK:]).astype(BF16)
            cs[h, ci[dc]] = rr[u][0:DK, :DK]
            os_[pl.ds(r0[dc], CH), h * DK:(h + 1) * DK] = rr[u][DK:DK + CH, :DK]
        return 0

    lax.fori_loop(0, TD // (CH * CU), local, 0)

    for ci in range(TD // CH):
        r0 = ci * CH
        gl_all = gls[ci]
        for h in range(DH):
            st = state[h]
            rr = jnp.dot(mqs[h, ci], st.astype(BF16), preferred_element_type=F32)
            state[h] = st * gl_all[:, DH + h:DH + h + 1] - rr[:DK] + cs[h, ci]
            o = rr[DK:] + os_[r0:r0 + CH, h * DK:(h + 1) * DK]
            o = o * lax.rsqrt(jnp.mean(o * o, axis=-1, keepdims=True) + EPS) * nw
            o = o * sz_ref[0, r0:r0 + CH, h * DK:(h + 1) * DK]
            o_ref[0, r0:r0 + CH, h * DK:(h + 1) * DK] = o.astype(o_ref.dtype)


def _deltanet(dx, sz, gb, gbt, conv_w, dn_norm_w):
    B, S, _ = dx.shape
    nt = S // TD
    return pl.pallas_call(
        _dn_kernel,
        grid=(B, nt),
        in_specs=[pl.BlockSpec((1, TD, 3 * DW), lambda b, t: (b, t, 0)),
                  pl.BlockSpec((1, TD, DW), lambda b, t: (b, t, 0)),
                  pl.BlockSpec((1, TD, LANES), lambda b, t: (b, t, 0)),
                  pl.BlockSpec((1, TD // CH, 8, CH), lambda b, t: (b, t, 0, 0)),
                  pl.BlockSpec((CONV, 3 * DW), lambda b, t: (0, 0)),
                  pl.BlockSpec((1, DK), lambda b, t: (0, 0))],
        out_specs=pl.BlockSpec((1, TD, DW), lambda b, t: (b, t, 0)),
        out_shape=jax.ShapeDtypeStruct((B, S, DW), BF16),
        scratch_shapes=[pltpu.VMEM((TD + 8, 3 * DW), F32),
                        pltpu.VMEM((TD, 3 * DW), F32),
                        pltpu.VMEM((DH, DK, DK), F32),
                        pltpu.VMEM((DH, TD // CH, DK + CH, DK), BF16),
                        pltpu.VMEM((DH, TD // CH, DK, DK), F32),
                        pltpu.VMEM((TD, DW), F32),
                        pltpu.VMEM((TD // CH, 1, LANES), F32)],
        compiler_params=pltpu.CompilerParams(
            dimension_semantics=("parallel", "arbitrary"), vmem_limit_bytes=VMEM_LIMIT),
        name="deltanet",
    )(dx, sz, gb, gbt, conv_w, dn_norm_w.reshape(1, DK))


def _outproj_kernel(om_ref, od_ref, x_ref, mod_ref, pw_ref, wt_ref, wb_ref, o_ref):
    y = jnp.dot(om_ref[0], wt_ref[...], preferred_element_type=F32)
    y = y + jnp.dot(od_ref[0], wb_ref[...], preferred_element_type=F32)
    yn = y * lax.rsqrt(jnp.mean(y * y, axis=-1, keepdims=True) + EPS) * pw_ref[...]
    o_ref[0] = x_ref[0] + mod_ref[0, 2:3, :] * yn


def _outproj(om, od, x, mod3, post_w, w_out):
    B, S, _ = x.shape
    wb = w_out.astype(BF16)
    row = lambda w: pl.BlockSpec((1, TO, w), lambda b, t: (b, t, 0))
    return pl.pallas_call(
        _outproj_kernel,
        grid=(B, S // TO),
        in_specs=[row(MW), row(DW), row(D),
                  pl.BlockSpec((1, 3, D), lambda b, t: (b, 0, 0)),
                  pl.BlockSpec((1, D), lambda b, t: (0, 0)),
                  pl.BlockSpec((MW, D), lambda b, t: (0, 0)),
                  pl.BlockSpec((DW, D), lambda b, t: (0, 0))],
        out_specs=row(D),
        out_shape=jax.ShapeDtypeStruct((B, S, D), F32),
        compiler_params=pltpu.CompilerParams(
            dimension_semantics=("parallel", "parallel"), vmem_limit_bytes=VMEM_LIMIT),
        name="outproj",
    )(om, od, x, mod3, post_w.reshape(1, D), wb[:MW], wb[MW:])


def kernel(x, c, ada_w, ada_b, pre_norm_w, post_norm_w, w_in, conv_w, a_log, dt_bias, dn_norm_w, w_out):
    B, S, _ = x.shape
    assert S % TD == 0 and S % TO == 0 and S % (BLK * PVG) == 0 and S % TM == 0 and TM % BLK == 0
    depth = ada_w.shape[0]
    for l in range(depth):
        mod3 = _mod(c, ada_w[l], ada_b[l]).reshape(B, 3, D)
        q, k, bias, vt, sg, dx, sz, gb, gbt = _inproj(x, mod3, pre_norm_w[l], w_in[l], a_log[l], dt_bias[l])
        om = _moba(q, k, vt, bias, sg)
        od = _deltanet(dx, sz, gb, gbt, conv_w[l], dn_norm_w[l])
        x = _outproj(om, od, x, mod3, post_norm_w[l], w_out[l])
    return x
```

```python
import functools
import math

import jax
import jax.numpy as jnp
from jax import lax
from jax.experimental import pallas as pl
from jax.experimental.pallas import tpu as pltpu

F32 = jnp.float32
BF16 = jnp.bfloat16
HI = lax.Precision.HIGHEST

D = 1024
MH, MD = 8, 64
MW = MH * MD
BLK = 256
TOPK = 3
ROPE_D = 16
ROPE_THETA = 500000.0
DH, DK = 4, 128
DW = DH * DK
CONV = 4
CH = 64
EPS = 1e-6
NEG = -1e30
LOWEST = -3.0e38
LOG2E = math.log2(math.e)
PVG = 4

LANES = 128
VMEM_LIMIT = 56 * 1024 * 1024

TM = 512
NBT = TM // BLK
TD = 512
CU = 8
TO = 1024

NT = (((1,), (1,)), ((), ()))
TN = (((0,), (0,)), ((), ()))


def _silu(v):
    h = 0.5 * v
    return h + h * jnp.tanh(h)


def _split_bf16(p):
    hi = p.astype(BF16).astype(F32)
    lo = (p - hi).astype(BF16).astype(F32)
    return hi, lo


def _split3_bf16(p):
    a, r = _split_bf16(p)
    c = (p - a - r).astype(BF16).astype(F32)
    return a, r, c


def _mod_kernel(c_ref, w_ref, b_ref, o_ref):
    c = c_ref[...]
    o_ref[...] = jnp.dot(_silu(c), w_ref[...], precision=HI, preferred_element_type=F32) + b_ref[...]


def _mod(c, ada_w, ada_b):
    B = c.shape[0]
    tn = 512
    return pl.pallas_call(
        _mod_kernel,
        grid=(3 * D // tn,),
        in_specs=[pl.BlockSpec((B, D), lambda j: (0, 0)),
                  pl.BlockSpec((D, tn), lambda j: (0, j)),
                  pl.BlockSpec((1, tn), lambda j: (0, j))],
        out_specs=pl.BlockSpec((B, tn), lambda j: (0, j)),
        out_shape=jax.ShapeDtypeStruct((B, 3 * D), F32),
        name="mod",
    )(c, ada_w, ada_b.reshape(1, 3 * D))


def _inproj_kernel(x_ref, mod_ref, pw_ref, wqk_ref, wvt_ref, wg_ref, wd_ref, wz_ref, ws_ref,
                   rc_ref, rs1_ref, rs2_ref, alog_ref, dtb_ref,
                   q_ref, k_ref, bias_ref, vt_ref, sg_ref, dx_ref, sz_ref, gb_ref, gbt_ref, kms, *, nb):
    t = pl.program_id(1)
    x = x_ref[0]
    shift = mod_ref[0, 0:1, :]
    scale = mod_ref[0, 1:2, :]
    y = x * lax.rsqrt(jnp.mean(x * x, axis=-1, keepdims=True) + EPS) * pw_ref[...]
    h = (y * (1.0 + scale) + shift).astype(BF16)

    rc, rs1, rs2 = rc_ref[...], rs1_ref[...], rs2_ref[...]

    def rope(t):
        tm1 = pltpu.roll(t, MW - ROPE_D // 2, axis=1)
        tp1 = pltpu.roll(t, ROPE_D // 2, axis=1)
        parts = []
        for n in range(MW // LANES):
            sl = slice(n * LANES, (n + 1) * LANES)
            parts.append(t[:, sl] * rc + tm1[:, sl] * rs1 + tp1[:, sl] * rs2)
        return jnp.concatenate(parts, axis=1)

    qk = jnp.dot(h, wqk_ref[...], preferred_element_type=F32)
    q = rope(qk[:, :MW])
    k = rope(qk[:, MW:])
    k_ref[0] = k.astype(BF16)
    q_ref[0] = (q * (MD ** -0.5 * LOG2E)).astype(BF16)

    vt = lax.dot_general(wvt_ref[...], h, NT, preferred_element_type=F32)
    for j in range(NBT):
        vt_ref[0, j] = vt[:, j * BLK:(j + 1) * BLK].astype(BF16)

    sg_ref[0] = _silu(jnp.dot(h, wg_ref[...], preferred_element_type=F32))

    @pl.when(t == 0)
    def _():
        kms[...] = jnp.zeros_like(kms)

    for j in range(NBT):
        kms[pl.ds(t * NBT + j, 1), :] = jnp.mean(k[j * BLK:(j + 1) * BLK], axis=0, keepdims=True)
    km = kms[...]
    wl = lax.broadcasted_iota(jnp.int32, (1, MW), 1)
    kmh = jnp.concatenate([jnp.where((wl >= hh * MD) & (wl < (hh + 1) * MD), km, 0.0) for hh in range(MH)], axis=0)
    kh, kl = _split_bf16(kmh)
    qh, ql = _split_bf16(q)
    gate = lax.dot_general(jnp.concatenate([kh, kl, kh], axis=1).astype(BF16),
                           jnp.concatenate([qh, qh, ql], axis=1).astype(BF16), NT,
                           preferred_element_type=F32)
    blk = lax.broadcasted_iota(jnp.int32, (MH, nb, TM), 1)
    qpos = lax.broadcasted_iota(jnp.int32, (MH, nb, TM), 2)
    own = t * NBT + sum((qpos >= j * BLK).astype(jnp.int32) for j in range(1, NBT))
    past = blk < own
    g3 = jnp.where(past, gate.reshape(MH, nb, TM), NEG)
    sel = jnp.zeros((MH, nb, TM), jnp.bool_)
    for _ in range(min(TOPK, nb)):
        gmax = jnp.max(g3, axis=1, keepdims=True)
        idx = jnp.min(jnp.where(g3 == gmax, blk, nb), axis=1, keepdims=True)
        pick = blk == idx
        sel = jnp.logical_or(sel, pick)
        g3 = jnp.where(pick, LOWEST, g3)
    bias_ref[0] = jnp.where(jnp.logical_or(jnp.logical_and(sel, past), blk == own), 0.0, NEG)

    s = jnp.dot(h, ws_ref[...], preferred_element_type=F32)
    lane = lax.broadcasted_iota(jnp.int32, s.shape, 1)
    beta = jax.nn.sigmoid(s)
    z = s + dtb_ref[...]
    softplus = jnp.maximum(z, 0.0) + jnp.log1p(jnp.exp(-jnp.abs(z)))
    g = -jnp.exp(alog_ref[...]) * softplus
    gb = jnp.where(lane < DH, beta, jnp.where(lane < 2 * DH, g, 0.0))
    gb_ref[0] = gb
    gbt = gb.T[0:8, :]
    for cidx in range(TM // CH):
        gbt_ref[0, cidx] = gbt[:, cidx * CH:(cidx + 1) * CH]

    sz_ref[0] = _silu(jnp.dot(h, wz_ref[...], preferred_element_type=F32))
    dx_ref[0] = jnp.dot(h, wd_ref[...], preferred_element_type=F32)


def _inproj(x, mod3, pre_w, w_in, a_log, dt_bias):
    B, S, _ = x.shape
    nb = S // BLK
    nt = S // TM
    wb = w_in.astype(BF16)
    wqk = wb[:, 0:2 * MW]
    wvt = wb[:, 2 * MW:3 * MW].T
    wg = wb[:, 3 * MW:4 * MW]
    wd = wb[:, 4 * MW:4 * MW + 3 * DW]
    wz = wb[:, 4 * MW + 3 * DW:4 * MW + 4 * DW]
    ws = jnp.pad(wb[:, 4 * MW + 4 * DW:], ((0, 0), (0, LANES - 2 * DH)))
    alog = jnp.pad(a_log.astype(F32), (DH, LANES - 2 * DH)).reshape(1, LANES)
    dtb = jnp.pad(dt_bias.astype(F32), (DH, LANES - 2 * DH)).reshape(1, LANES)

    half = ROPE_D // 2
    inv_freq = jnp.power(ROPE_THETA, -jnp.arange(0, ROPE_D, 2, dtype=F32) / ROPE_D)
    ang = jnp.arange(S, dtype=jnp.int32).astype(F32)[:, None] * inv_freq[None, :]
    cos, sin = jnp.cos(ang), jnp.sin(ang)
    one = jnp.ones((S, MD - ROPE_D), F32)
    zero = jnp.zeros((S, MD - ROPE_D), F32)
    zh = jnp.zeros((S, half), F32)
    rc = jnp.tile(jnp.concatenate([cos, cos, one], axis=1), (1, 2))
    rs1 = jnp.tile(jnp.concatenate([-sin, zh, zero], axis=1), (1, 2))
    rs2 = jnp.tile(jnp.concatenate([zh, sin, zero], axis=1), (1, 2))

    const = lambda shape: pl.BlockSpec(shape, lambda b, t: (0,) * len(shape))
    row = lambda w: pl.BlockSpec((1, TM, w), lambda b, t: (b, t, 0))
    tab = pl.BlockSpec((TM, LANES), lambda b, t: (t, 0))
    out_shapes = [
        jax.ShapeDtypeStruct((B, S, MW), BF16),
        jax.ShapeDtypeStruct((B, S, MW), BF16),
        jax.ShapeDtypeStruct((B, MH, nb, S), F32),
        jax.ShapeDtypeStruct((B, nb, MW, BLK), BF16),
        jax.ShapeDtypeStruct((B, S, MW), F32),
        jax.ShapeDtypeStruct((B, S, 3 * DW), F32),
        jax.ShapeDtypeStruct((B, S, DW), F32),
        jax.ShapeDtypeStruct((B, S, LANES), F32),
        jax.ShapeDtypeStruct((B, S // CH, 8, CH), F32),
    ]
    out_specs = [
        row(MW), row(MW),
        pl.BlockSpec((1, MH, nb, TM), lambda b, t: (b, 0, 0, t)),
        pl.BlockSpec((1, NBT, MW, BLK), lambda b, t: (b, t, 0, 0)),
        row(MW), row(3 * DW), row(DW), row(LANES),
        pl.BlockSpec((1, TM // CH, 8, CH), lambda b, t: (b, t, 0, 0)),
    ]
    return pl.pallas_call(
        functools.partial(_inproj_kernel, nb=nb),
        grid=(B, nt),
        in_specs=[row(D), pl.BlockSpec((1, 3, D), lambda b, t: (b, 0, 0)), const((1, D)),
                  const((D, 2 * MW)), const((MW, D)), const((D, MW)), const((D, 3 * DW)),
                  const((D, DW)), const((D, LANES)), tab, tab, tab,
                  const((1, LANES)), const((1, LANES))],
        out_specs=out_specs,
        out_shape=out_shapes,
        scratch_shapes=[pltpu.VMEM((nb, MW), F32)],
        compiler_params=pltpu.CompilerParams(
            dimension_semantics=("parallel", "arbitrary"), vmem_limit_bytes=VMEM_LIMIT),
        name="inproj",
    )(x, mod3, pre_w.reshape(1, D), wqk, wvt, wg, wd, wz, ws, rc, rs1, rs2, alog, dtb)


NPAIR = MW // LANES
ONES = 16


def _moba_kernel(q_ref, k_ref, vt_ref, bias_ref, sg_ref, qn_ref, biasn_ref, o_ref,
                 pm_ref, s0_ref, s1_ref, qm_ref, acc_ref, m0_ref):
    i = pl.program_id(1)
    lane = lax.broadcasted_iota(jnp.int32, (1, LANES), 1)
    head_lanes = [lane < MD, lane >= MD]

    def fold8(a):
        return a.reshape(BLK // 8, 8, BLK)

    @pl.when(i == 0)
    def _():
        kpos = lax.broadcasted_iota(jnp.int32, (BLK, BLK), 0)
        qpos = lax.broadcasted_iota(jnp.int32, (BLK, BLK), 1)
        pm_ref[0] = jnp.zeros((BLK, BLK), F32)
        pm_ref[1] = jnp.where(kpos <= qpos, 0.0, NEG)
        pm_ref[2] = jnp.full((BLK, BLK), NEG, F32)

    ngroups = i // PVG + 1

    s_refs = (s0_ref, s1_ref)

    nb = bias_ref.shape[2]
    ones_rows = jnp.ones((ONES, BLK), BF16)

    def mask_q(slot, qp):
        for h in range(2):
            qm_ref[slot, h] = jnp.where(head_lanes[h], qp, jnp.zeros_like(qp))
        return [qm_ref.at[slot, h] for h in range(2)]

    def scores(pr, qs, j, mx, tile, bias, hoff):
        kj = k_ref[0, pl.ds(pl.multiple_of(j * BLK, BLK), BLK), pr * LANES:(pr + 1) * LANES]
        pm = pm_ref[jnp.clip(j - tile, -1, 1) + 1]
        out = []
        for h in range(2):
            s = lax.dot_general(kj, qs[h][...], NT, preferred_element_type=F32) + pm
            s_refs[pr % 2][h, j] = s
            out.append(jnp.maximum(mx[h], jnp.max(fold8(s), axis=0) + bias[0, hoff + h, pl.ds(j, 1), :]))
        return tuple(out)

    def probs_pv(pr, m, j):
        for h in range(2):
            p = jnp.exp2(s_refs[pr % 2][h, j] - (m[h] - bias_ref[0, 2 * pr + h, pl.ds(j, 1), :]))
            vth = vt_ref[0, j, pr * LANES + h * MD:pr * LANES + (h + 1) * MD, :]
            acc_ref[h] += jnp.dot(jnp.concatenate([vth, ones_rows], axis=0), p.astype(BF16),
                                  preferred_element_type=F32)

    def finalize(pr):
        ot = jnp.concatenate([acc_ref[h, 0:MD, :] / acc_ref[h, MD:MD + 1, :] for h in range(2)], axis=0)
        sl = slice(pr * LANES, (pr + 1) * LANES)
        o_ref[0, :, sl] = (ot.T * sg_ref[0, :, sl]).astype(o_ref.dtype)

    def mx_init():
        return tuple(jnp.full((8, BLK), NEG, F32) for _ in range(2))

    def colmax(mx):
        return [jnp.max(mx[h], axis=0, keepdims=True) for h in range(2)]

    @pl.when(i == 0)
    def _():
        qs = mask_q(0, q_ref[0, :, 0:LANES])

        def body(t, mx):
            for dj in range(PVG):
                mx = scores(0, qs, t * PVG + dj, mx, i, bias_ref, 0)
            return mx

        cm = colmax(lax.fori_loop(0, ngroups, body, mx_init()))
        for h in range(2):
            m0_ref[h] = cm[h]

    m = {0: [m0_ref[h] for h in range(2)]}
    for stage in range(1, NPAIR):
        p1, p2 = stage, stage - 1
        qs = mask_q(p1 % 2, q_ref[0, :, p1 * LANES:(p1 + 1) * LANES])
        acc_ref[...] = jnp.zeros_like(acc_ref)

        def body(t, mx):
            for dj in range(PVG):
                mx = scores(p1, qs, t * PVG + dj, mx, i, bias_ref, 2 * p1)
                probs_pv(p2, m[p2], t * PVG + dj)
            return mx

        m[p1] = colmax(lax.fori_loop(0, ngroups, body, mx_init()))
        finalize(p2)

    last = NPAIR - 1
    acc_ref[...] = jnp.zeros_like(acc_ref)

    @pl.when(i < nb - 1)
    def _():
        qs = mask_q(0, qn_ref[0])

        def body(t, mx):
            for dj in range(PVG):
                mx = scores(0, qs, t * PVG + dj, mx, i + 1, biasn_ref, 0)
                probs_pv(last, m[last], t * PVG + dj)
            return mx

        def extra(t, mx):
            for dj in range(PVG):
                mx = scores(0, qs, t * PVG + dj, mx, i + 1, biasn_ref, 0)
            return mx

        mx = lax.fori_loop(0, ngroups, body, mx_init())
        cm = colmax(lax.fori_loop(ngroups, (i + 1) // PVG + 1, extra, mx))
        for h in range(2):
            m0_ref[h] = cm[h]

    @pl.when(i == nb - 1)
    def _():
        def body(t, carry):
            for dj in range(PVG):
                probs_pv(last, m[last], t * PVG + dj)
            return carry

        lax.fori_loop(0, ngroups, body, 0)

    finalize(last)


def _moba(q, k, vt, bias, sg):
    B, S, _ = q.shape
    nb = S // BLK
    return pl.pallas_call(
        _moba_kernel,
        grid=(B, nb),
        in_specs=[pl.BlockSpec((1, BLK, MW), lambda b, i: (b, i, 0)),
                  pl.BlockSpec((1, S, MW), lambda b, i: (b, 0, 0)),
                  pl.BlockSpec((1, nb, MW, BLK), lambda b, i: (b, 0, 0, 0)),
                  pl.BlockSpec((1, MH, nb, BLK), lambda b, i: (b, 0, 0, i)),
                  pl.BlockSpec((1, BLK, MW), lambda b, i: (b, i, 0)),
                  pl.BlockSpec((1, BLK, LANES), lambda b, i: (b, jnp.minimum(i + 1, nb - 1), 0)),
                  pl.BlockSpec((1, 2, nb, BLK), lambda b, i: (b, 0, 0, jnp.minimum(i + 1, nb - 1)))],
        out_specs=pl.BlockSpec((1, BLK, MW), lambda b, i: (b, i, 0)),
        out_shape=jax.ShapeDtypeStruct((B, S, MW), BF16),
        scratch_shapes=[pltpu.VMEM((3, BLK, BLK), F32),
                        pltpu.VMEM((2, nb, BLK, BLK), F32),
                        pltpu.VMEM((2, nb, BLK, BLK), F32),
                        pltpu.VMEM((2, 2, BLK, LANES), BF16),
                        pltpu.VMEM((2, MD + ONES, BLK), F32),
                        pltpu.VMEM((2, 1, BLK), F32)],
        compiler_params=pltpu.CompilerParams(
            dimension_semantics=("parallel", "arbitrary"), vmem_limit_bytes=VMEM_LIMIT),
        name="moba",
    )(q, k, vt, bias, sg, q, bias)


def _dn_kernel(x_ref, sz_ref, gb_ref, gbt_ref, cw_ref, nw_ref, o_ref, xbuf, qkv, state, mqs, cs, os_, gls):
    t = pl.program_id(1)
    halo = 8

    @pl.when(t == 0)
    def _():
        xbuf[0:halo, :] = jnp.zeros((halo, 3 * DW), F32)
        state[...] = jnp.zeros_like(state)

    @pl.when(t > 0)
    def _():
        xbuf[0:halo, :] = xbuf[TD:TD + halo, :]

    xbuf[halo:halo + TD, :] = x_ref[0]

    for n in range(3 * DH):
        sl = slice(n * DK, (n + 1) * DK)
        x0 = xbuf[:, sl]
        x1 = pltpu.roll(x0, 1, axis=0)
        a = cw_ref[3:4, sl] * x0 + cw_ref[2:3, sl] * x1
        b = cw_ref[1:2, sl] * x0 + cw_ref[0:1, sl] * x1
        y = (a + pltpu.roll(b, 2, axis=0))[halo:halo + TD]
        y = _silu(y)
        if n < 2 * DH:
            y = y * lax.rsqrt(jnp.sum(y * y, axis=-1, keepdims=True) + EPS)
        if n < DH:
            y = y * (DK ** -0.5)
        qkv[:, sl] = y

    r = lax.broadcasted_iota(jnp.int32, (CH, CH), 0)
    c = lax.broadcasted_iota(jnp.int32, (CH, CH), 1)
    tril = r >= c
    r2 = lax.broadcasted_iota(jnp.int32, (CH, 2 * CH), 0)
    c2 = lax.broadcasted_iota(jnp.int32, (CH, 2 * CH), 1)
    c4 = lax.broadcasted_iota(jnp.int32, (CH, 2 * DK), 1)
    keep = jnp.logical_or(jnp.logical_and(c2 < CH, c2 >= r2), c2 - CH > r2)
    ones_l3 = jnp.concatenate([tril.astype(F32)] * 3, axis=1).astype(BF16)
    ones_u3 = jnp.concatenate([(r <= c).astype(F32)] * 3, axis=0).astype(BF16)
    nw = nw_ref[...]

    def local(it, _):
        units = [(dc, h) for dc in range(CU) for h in range(DH)]
        ci = [it * CU + dc for dc in range(CU)]
        r0 = [pl.multiple_of(c_ * CH, CH) for c_ in ci]
        gcol = [gb_ref[0, pl.ds(r0[dc], CH), :] for dc in range(CU)]
        grow = [gbt_ref[0, ci[dc]] for dc in range(CU)]
        gc = [jnp.dot(ones_l3, jnp.concatenate(_split3_bf16(g), axis=0).astype(BF16),
                      preferred_element_type=F32) for g in gcol]
        gr = [jnp.dot(jnp.concatenate(_split3_bf16(g), axis=1).astype(BF16), ones_u3,
                      preferred_element_type=F32) for g in grow]
        q = [qkv[pl.ds(r0[dc], CH), h * DK:(h + 1) * DK] for dc, h in units]
        k = [qkv[pl.ds(r0[dc], CH), DW + h * DK:DW + (h + 1) * DK] for dc, h in units]
        v = [qkv[pl.ds(r0[dc], CH), 2 * DW + h * DK:2 * DW + (h + 1) * DK] for dc, h in units]
        beta = [gcol[dc][:, h:h + 1] for dc, h in units]
        gcum = [gc[dc][:, DH + h:DH + h + 1] for dc, h in units]
        glast = [g[CH - 1:CH, :] for g in gcum]
        us = range(len(units))
        eg = [jnp.exp(g) for g in gcum]
        kb = [k[u] * beta[u] for u in us]
        dT = []
        for u, (dc, h) in enumerate(units):
            g2 = jnp.concatenate([gr[dc][DH + h:DH + h + 1, :]] * 2, axis=1)
            dT.append(jnp.where(keep, jnp.exp(jnp.where(keep, g2 - gcum[u], 0.0)), 0.0))
        aqt = [lax.dot_general(k[u].astype(BF16), jnp.concatenate([q[u], kb[u]], axis=0).astype(BF16), NT,
                               preferred_element_type=F32) for u in us]
        w = [jnp.concatenate([k[u] * jnp.exp(glast[u] - gcum[u]), aqt[u] * dT[u]], axis=1) for u in us]
        for lvl in range(6):
            for u in us:
                wh, wlo = _split_bf16(w[u])
                ph = pltpu.roll(wh[:, DK:], CH, axis=1)
                lhs = jnp.concatenate([jnp.where(c2 < CH, ph, wlo[:, DK:]), ph[:, :CH]], axis=1)
                res = jnp.dot(lhs.astype(BF16), jnp.concatenate([wh, wh, wlo], axis=0).astype(BF16),
                              preferred_element_type=F32)
                w[u] = jnp.where(c4 < 3 * CH, w[u] - res if lvl == 0 else w[u] + res, res)
        x0 = [jnp.concatenate([v[u] * beta[u], kb[u] * eg[u]], axis=1).astype(BF16) for u in us]
        rr = [jnp.dot(w[u][:, :3 * CH].T.astype(BF16), x0[u], preferred_element_type=F32) for u in us]
        for dc in range(CU):
            gls[ci[dc]] = jnp.exp(gc[dc][CH - 1:CH, :])
        for u, (dc, h) in enumerate(units):
            mqs[h, ci[dc], 0:DK, :] = rr[u][0:DK, DK:].astype(BF16)
            mqs[h, ci[dc], DK:DK + CH, :] = (q[u] * eg[u] - rr[u][DK:DK + CH, DK:]).astype(BF16)
            cs[h, ci[dc]] = rr[u][0:DK, :DK]
            os_[pl.ds(r0[dc], CH), h * DK:(h + 1) * DK] = rr[u][DK:DK + CH, :DK]
        return 0

    lax.fori_loop(0, TD // (CH * CU), local, 0)

    for ci in range(TD // CH):
        r0 = ci * CH
        gl_all = gls[ci]
        for h in range(DH):
            st = state[h]
            rr = jnp.dot(mqs[h, ci], st.astype(BF16), preferred_element_type=F32)
            state[h] = st * gl_all[:, DH + h:DH + h + 1] - rr[:DK] + cs[h, ci]
            o = rr[DK:] + os_[r0:r0 + CH, h * DK:(h + 1) * DK]
            o = o * lax.rsqrt(jnp.mean(o * o, axis=-1, keepdims=True) + EPS) * nw
            o = o * sz_ref[0, r0:r0 + CH, h * DK:(h + 1) * DK]
            o_ref[0, r0:r0 + CH, h * DK:(h + 1) * DK] = o.astype(o_ref.dtype)


def _deltanet(dx, sz, gb, gbt, conv_w, dn_norm_w):
    B, S, _ = dx.shape
    nt = S // TD
    return pl.pallas_call(
        _dn_kernel,
        grid=(B, nt),
        in_specs=[pl.BlockSpec((1, TD, 3 * DW), lambda b, t: (b, t, 0)),
                  pl.BlockSpec((1, TD, DW), lambda b, t: (b, t, 0)),
                  pl.BlockSpec((1, TD, LANES), lambda b, t: (b, t, 0)),
                  pl.BlockSpec((1, TD // CH, 8, CH), lambda b, t: (b, t, 0, 0)),
                  pl.BlockSpec((CONV, 3 * DW), lambda b, t: (0, 0)),
                  pl.BlockSpec((1, DK), lambda b, t: (0, 0))],
        out_specs=pl.BlockSpec((1, TD, DW), lambda b, t: (b, t, 0)),
        out_shape=jax.ShapeDtypeStruct((B, S, DW), BF16),
        scratch_shapes=[pltpu.VMEM((TD + 8, 3 * DW), F32),
                        pltpu.VMEM((TD, 3 * DW), F32),
                        pltpu.VMEM((DH, DK, DK), F32),
                        pltpu.VMEM((DH, TD // CH, DK + CH, DK), BF16),
                        pltpu.VMEM((DH, TD // CH, DK, DK), F32),
                        pltpu.VMEM((TD, DW), F32),
                        pltpu.VMEM((TD // CH, 1, LANES), F32)],
        compiler_params=pltpu.CompilerParams(
            dimension_semantics=("parallel", "arbitrary"), vmem_limit_bytes=VMEM_LIMIT),
        name="deltanet",
    )(dx, sz, gb, gbt, conv_w, dn_norm_w.reshape(1, DK))


def _outproj_kernel(om_ref, od_ref, x_ref, mod_ref, pw_ref, wt_ref, wb_ref, o_ref):
    y = jnp.dot(om_ref[0], wt_ref[...], preferred_element_type=F32)
    y = y + jnp.dot(od_ref[0], wb_ref[...], preferred_element_type=F32)
    yn = y * lax.rsqrt(jnp.mean(y * y, axis=-1, keepdims=True) + EPS) * pw_ref[...]
    o_ref[0] = x_ref[0] + mod_ref[0, 2:3, :] * yn


def _outproj(om, od, x, mod3, post_w, w_out):
    B, S, _ = x.shape
    wb = w_out.astype(BF16)
    row = lambda w: pl.BlockSpec((1, TO, w), lambda b, t: (b, t, 0))
    return pl.pallas_call(
        _outproj_kernel,
        grid=(B, S // TO),
        in_specs=[row(MW), row(DW), row(D),
                  pl.BlockSpec((1, 3, D), lambda b, t: (b, 0, 0)),
                  pl.BlockSpec((1, D), lambda b, t: (0, 0)),
                  pl.BlockSpec((MW, D), lambda b, t: (0, 0)),
                  pl.BlockSpec((DW, D), lambda b, t: (0, 0))],
        out_specs=row(D),
        out_shape=jax.ShapeDtypeStruct((B, S, D), F32),
        compiler_params=pltpu.CompilerParams(
            dimension_semantics=("parallel", "parallel"), vmem_limit_bytes=VMEM_LIMIT),
        name="outproj",
    )(om, od, x, mod3, post_w.reshape(1, D), wb[:MW], wb[MW:])


def kernel(x, c, ada_w, ada_b, pre_norm_w, post_norm_w, w_in, conv_w, a_log, dt_bias, dn_norm_w, w_out):
    B, S, _ = x.shape
    assert S % TD == 0 and S % TO == 0 and S % (BLK * PVG) == 0 and S % TM == 0 and TM % BLK == 0
    depth = ada_w.shape[0]
    for l in range(depth):
        mod3 = _mod(c, ada_w[l], ada_b[l]).reshape(B, 3, D)
        q, k, bias, vt, sg, dx, sz, gb, gbt = _inproj(x, mod3, pre_norm_w[l], w_in[l], a_log[l], dt_bias[l])
        om = _moba(q, k, vt, bias, sg)
        od = _deltanet(dx, sz, gb, gbt, conv_w[l], dn_norm_w[l])
        x = _outproj(om, od, x, mod3, post_norm_w[l], w_out[l])
    return x
```

```python
import functools
import math

import jax
import jax.numpy as jnp
from jax import lax
from jax.experimental import pallas as pl
from jax.experimental.pallas import tpu as pltpu

F32 = jnp.float32
BF16 = jnp.bfloat16
HI = lax.Precision.HIGHEST

D = 1024
MH, MD = 8, 64
MW = MH * MD
BLK = 256
TOPK = 3
ROPE_D = 16
ROPE_THETA = 500000.0
DH, DK = 4, 128
DW = DH * DK
CONV = 4
CH = 64
EPS = 1e-6
NEG = -1e30
LOWEST = -3.0e38
LOG2E = math.log2(math.e)
PVG = 4

LANES = 128
VMEM_LIMIT = 56 * 1024 * 1024

TM = 512
NBT = TM // BLK
TD = 512
CU = 8
TO = 1024

NT = (((1,), (1,)), ((), ()))
TN = (((0,), (0,)), ((), ()))


def _silu(v):
    h = 0.5 * v
    return h + h * jnp.tanh(h)


def _split_bf16(p):
    hi = p.astype(BF16).astype(F32)
    lo = (p - hi).astype(BF16).astype(F32)
    return hi, lo


def _split3_bf16(p):
    a, r = _split_bf16(p)
    c = (p - a - r).astype(BF16).astype(F32)
    return a, r, c


def _mod_kernel(c_ref, w_ref, b_ref, o_ref):
    c = c_ref[...]
    o_ref[...] = jnp.dot(_silu(c), w_ref[...], precision=HI, preferred_element_type=F32) + b_ref[...]


def _mod(c, ada_w, ada_b):
    B = c.shape[0]
    tn = 512
    return pl.pallas_call(
        _mod_kernel,
        grid=(3 * D // tn,),
        in_specs=[pl.BlockSpec((B, D), lambda j: (0, 0)),
                  pl.BlockSpec((D, tn), lambda j: (0, j)),
                  pl.BlockSpec((1, tn), lambda j: (0, j))],
        out_specs=pl.BlockSpec((B, tn), lambda j: (0, j)),
        out_shape=jax.ShapeDtypeStruct((B, 3 * D), F32),
        name="mod",
    )(c, ada_w, ada_b.reshape(1, 3 * D))


def _inproj_kernel(x_ref, mod_ref, pw_ref, wqk_ref, wvt_ref, wg_ref, wd_ref, wz_ref, ws_ref,
                   rc_ref, rs1_ref, rs2_ref, alog_ref, dtb_ref,
                   q_ref, k_ref, bias_ref, vt_ref, sg_ref, dx_ref, sz_ref, gb_ref, gbt_ref, kms, *, nb):
    t = pl.program_id(1)
    x = x_ref[0]
    shift = mod_ref[0, 0:1, :]
    scale = mod_ref[0, 1:2, :]
    y = x * lax.rsqrt(jnp.mean(x * x, axis=-1, keepdims=True) + EPS) * pw_ref[...]
    h = (y * (1.0 + scale) + shift).astype(BF16)

    rc, rs1, rs2 = rc_ref[...], rs1_ref[...], rs2_ref[...]

    def rope(t):
        tm1 = pltpu.roll(t, MW - ROPE_D // 2, axis=1)
        tp1 = pltpu.roll(t, ROPE_D // 2, axis=1)
        parts = []
        for n in range(MW // LANES):
            sl = slice(n * LANES, (n + 1) * LANES)
            parts.append(t[:, sl] * rc + tm1[:, sl] * rs1 + tp1[:, sl] * rs2)
        return jnp.concatenate(parts, axis=1)

    qk = jnp.dot(h, wqk_ref[...], preferred_element_type=F32)
    q = rope(qk[:, :MW])
    k = rope(qk[:, MW:])
    k_ref[0] = k.astype(BF16)
    q_ref[0] = (q * (MD ** -0.5 * LOG2E)).astype(BF16)

    vt = lax.dot_general(wvt_ref[...], h, NT, preferred_element_type=F32)
    for j in range(NBT):
        vt_ref[0, j] = vt[:, j * BLK:(j + 1) * BLK].astype(BF16)

    sg_ref[0] = _silu(jnp.dot(h, wg_ref[...], preferred_element_type=F32))

    @pl.when(t == 0)
    def _():
        kms[...] = jnp.zeros_like(kms)

    for j in range(NBT):
        kms[pl.ds(t * NBT + j, 1), :] = jnp.mean(k[j * BLK:(j + 1) * BLK], axis=0, keepdims=True)
    km = kms[...]
    wl = lax.broadcasted_iota(jnp.int32, (1, MW), 1)
    kmh = jnp.concatenate([jnp.where((wl >= hh * MD) & (wl < (hh + 1) * MD), km, 0.0) for hh in range(MH)], axis=0)
    kh, kl = _split_bf16(kmh)
    qh, ql = _split_bf16(q)
    gate = lax.dot_general(jnp.concatenate([kh, kl, kh], axis=1).astype(BF16),
                           jnp.concatenate([qh, qh, ql], axis=1).astype(BF16), NT,
                           preferred_element_type=F32)
    blk = lax.broadcasted_iota(jnp.int32, (MH, nb, TM), 1)
    qpos = lax.broadcasted_iota(jnp.int32, (MH, nb, TM), 2)
    own = t * NBT + sum((qpos >= j * BLK).astype(jnp.int32) for j in range(1, NBT))
    past = blk < own
    g3 = jnp.where(past, gate.reshape(MH, nb, TM), NEG)
    sel = jnp.zeros((MH, nb, TM), jnp.bool_)
    for _ in range(min(TOPK, nb)):
        gmax = jnp.max(g3, axis=1, keepdims=True)
        idx = jnp.min(jnp.where(g3 == gmax, blk, nb), axis=1, keepdims=True)
        pick = blk == idx
        sel = jnp.logical_or(sel, pick)
        g3 = jnp.where(pick, LOWEST, g3)
    bias_ref[0] = jnp.where(jnp.logical_or(jnp.logical_and(sel, past), blk == own), 0.0, NEG)

    s = jnp.dot(h, ws_ref[...], preferred_element_type=F32)
    lane = lax.broadcasted_iota(jnp.int32, s.shape, 1)
    beta = jax.nn.sigmoid(s)
    z = s + dtb_ref[...]
    softplus = jnp.maximum(z, 0.0) + jnp.log1p(jnp.exp(-jnp.abs(z)))
    g = -jnp.exp(alog_ref[...]) * softplus
    gb = jnp.where(lane < DH, beta, jnp.where(lane < 2 * DH, g, 0.0))
    gb_ref[0] = gb
    gbt = gb.T[0:8, :]
    for cidx in range(TM // CH):
        gbt_ref[0, cidx] = gbt[:, cidx * CH:(cidx + 1) * CH]

    sz_ref[0] = _silu(jnp.dot(h, wz_ref[...], preferred_element_type=F32))
    dx_ref[0] = jnp.dot(h, wd_ref[...], preferred_element_type=F32)


def _inproj(x, mod3, pre_w, w_in, a_log, dt_bias):
    B, S, _ = x.shape
    nb = S // BLK
    nt = S // TM
    wb = w_in.astype(BF16)
    wqk = wb[:, 0:2 * MW]
    wvt = wb[:, 2 * MW:3 * MW].T
    wg = wb[:, 3 * MW:4 * MW]
    wd = wb[:, 4 * MW:4 * MW + 3 * DW]
    wz = wb[:, 4 * MW + 3 * DW:4 * MW + 4 * DW]
    ws = jnp.pad(wb[:, 4 * MW + 4 * DW:], ((0, 0), (0, LANES - 2 * DH)))
    alog = jnp.pad(a_log.astype(F32), (DH, LANES - 2 * DH)).reshape(1, LANES)
    dtb = jnp.pad(dt_bias.astype(F32), (DH, LANES - 2 * DH)).reshape(1, LANES)

    half = ROPE_D // 2
    inv_freq = jnp.power(ROPE_THETA, -jnp.arange(0, ROPE_D, 2, dtype=F32) / ROPE_D)
    ang = jnp.arange(S, dtype=jnp.int32).astype(F32)[:, None] * inv_freq[None, :]
    cos, sin = jnp.cos(ang), jnp.sin(ang)
    one = jnp.ones((S, MD - ROPE_D), F32)
    zero = jnp.zeros((S, MD - ROPE_D), F32)
    zh = jnp.zeros((S, half), F32)
    rc = jnp.tile(jnp.concatenate([cos, cos, one], axis=1), (1, 2))
    rs1 = jnp.tile(jnp.concatenate([-sin, zh, zero], axis=1), (1, 2))
    rs2 = jnp.tile(jnp.concatenate([zh, sin, zero], axis=1), (1, 2))

    const = lambda shape: pl.BlockSpec(shape, lambda b, t: (0,) * len(shape))
    row = lambda w: pl.BlockSpec((1, TM, w), lambda b, t: (b, t, 0))
    tab = pl.BlockSpec((TM, LANES), lambda b, t: (t, 0))
    out_shapes = [
        jax.ShapeDtypeStruct((B, S, MW), BF16),
        jax.ShapeDtypeStruct((B, S, MW), BF16),
        jax.ShapeDtypeStruct((B, MH, nb, S), F32),
        jax.ShapeDtypeStruct((B, nb, MW, BLK), BF16),
        jax.ShapeDtypeStruct((B, S, MW), F32),
        jax.ShapeDtypeStruct((B, S, 3 * DW), F32),
        jax.ShapeDtypeStruct((B, S, DW), F32),
        jax.ShapeDtypeStruct((B, S, LANES), F32),
        jax.ShapeDtypeStruct((B, S // CH, 8, CH), F32),
    ]
    out_specs = [
        row(MW), row(MW),
        pl.BlockSpec((1, MH, nb, TM), lambda b, t: (b, 0, 0, t)),
        pl.BlockSpec((1, NBT, MW, BLK), lambda b, t: (b, t, 0, 0)),
        row(MW), row(3 * DW), row(DW), row(LANES),
        pl.BlockSpec((1, TM // CH, 8, CH), lambda b, t: (b, t, 0, 0)),
    ]
    return pl.pallas_call(
        functools.partial(_inproj_kernel, nb=nb),
        grid=(B, nt),
        in_specs=[row(D), pl.BlockSpec((1, 3, D), lambda b, t: (b, 0, 0)), const((1, D)),
                  const((D, 2 * MW)), const((MW, D)), const((D, MW)), const((D, 3 * DW)),
                  const((D, DW)), const((D, LANES)), tab, tab, tab,
                  const((1, LANES)), const((1, LANES))],
        out_specs=out_specs,
        out_shape=out_shapes,
        scratch_shapes=[pltpu.VMEM((nb, MW), F32)],
        compiler_params=pltpu.CompilerParams(
            dimension_semantics=("parallel", "arbitrary"), vmem_limit_bytes=VMEM_LIMIT),
        name="inproj",
    )(x, mod3, pre_w.reshape(1, D), wqk, wvt, wg, wd, wz, ws, rc, rs1, rs2, alog, dtb)


NPAIR = MW // LANES
ONES = 16


def _moba_kernel(q_ref, k_ref, vt_ref, bias_ref, sg_ref, qn_ref, biasn_ref, o_ref,
                 pm_ref, s0_ref, s1_ref, qm_ref, acc_ref, m0_ref):
    i = pl.program_id(1)
    lane = lax.broadcasted_iota(jnp.int32, (1, LANES), 1)
    head_lanes = [lane < MD, lane >= MD]

    def fold8(a):
        return a.reshape(BLK // 8, 8, BLK)

    @pl.when(i == 0)
    def _():
        kpos = lax.broadcasted_iota(jnp.int32, (BLK, BLK), 0)
        qpos = lax.broadcasted_iota(jnp.int32, (BLK, BLK), 1)
        pm_ref[0] = jnp.zeros((BLK, BLK), F32)
        pm_ref[1] = jnp.where(kpos <= qpos, 0.0, NEG)
        pm_ref[2] = jnp.full((BLK, BLK), NEG, F32)

    ngroups = i // PVG + 1

    s_refs = (s0_ref, s1_ref)

    nb = bias_ref.shape[2]
    ones_rows = jnp.ones((ONES, BLK), BF16)

    def mask_q(slot, qp):
        for h in range(2):
            qm_ref[slot, h] = jnp.where(head_lanes[h], qp, jnp.zeros_like(qp))
        return [qm_ref.at[slot, h] for h in range(2)]

    def scores(pr, qs, j, mx, tile, bias, hoff):
        kj = k_ref[0, pl.ds(pl.multiple_of(j * BLK, BLK), BLK), pr * LANES:(pr + 1) * LANES]
        pm = pm_ref[jnp.clip(j - tile, -1, 1) + 1]
        out = []
        for h in range(2):
            s = lax.dot_general(kj, qs[h][...], NT, preferred_element_type=F32) + pm
            s_refs[pr % 2][h, j] = s
            out.append(jnp.maximum(mx[h], jnp.max(fold8(s), axis=0) + bias[0, hoff + h, pl.ds(j, 1), :]))
        return tuple(out)

    def probs_pv(pr, m, j):
        for h in range(2):
            p = jnp.exp2(s_refs[pr % 2][h, j] - (m[h] - bias_ref[0, 2 * pr + h, pl.ds(j, 1), :]))
            vth = vt_ref[0, j, pr * LANES + h * MD:pr * LANES + (h + 1) * MD, :]
            acc_ref[pr, h] += jnp.dot(jnp.concatenate([vth, ones_rows], axis=0), p.astype(BF16),
                                      preferred_element_type=F32)

    def finalize(pr):
        ot = jnp.concatenate([acc_ref[pr, h, 0:MD, :] / acc_ref[pr, h, MD:MD + 1, :] for h in range(2)], axis=0)
        sl = slice(pr * LANES, (pr + 1) * LANES)
        o_ref[0, :, sl] = (ot.T * sg_ref[0, :, sl]).astype(o_ref.dtype)

    def mx_init():
        return tuple(jnp.full((8, BLK), NEG, F32) for _ in range(2))

    def colmax(mx):
        return [jnp.max(mx[h], axis=0, keepdims=True) for h in range(2)]

    @pl.when(i == 0)
    def _():
        qs = mask_q(0, q_ref[0, :, 0:LANES])

        def body(t, mx):
            for dj in range(PVG):
                mx = scores(0, qs, t * PVG + dj, mx, i, bias_ref, 0)
            return mx

        cm = colmax(lax.fori_loop(0, ngroups, body, mx_init()))
        for h in range(2):
            m0_ref[h] = cm[h]

    m = {0: [m0_ref[h] for h in range(2)]}
    acc_ref[...] = jnp.zeros_like(acc_ref)
    for stage in range(1, NPAIR):
        p1, p2 = stage, stage - 1
        qs = mask_q(p1 % 2, q_ref[0, :, p1 * LANES:(p1 + 1) * LANES])

        def body(t, mx):
            for dj in range(PVG):
                mx = scores(p1, qs, t * PVG + dj, mx, i, bias_ref, 2 * p1)
                probs_pv(p2, m[p2], t * PVG + dj)
            return mx

        m[p1] = colmax(lax.fori_loop(0, ngroups, body, mx_init()))

    last = NPAIR - 1

    @pl.when(i < nb - 1)
    def _():
        qs = mask_q(0, qn_ref[0])

        def body(t, mx):
            for dj in range(PVG):
                mx = scores(0, qs, t * PVG + dj, mx, i + 1, biasn_ref, 0)
                probs_pv(last, m[last], t * PVG + dj)
            return mx

        def extra(t, mx):
            for dj in range(PVG):
                mx = scores(0, qs, t * PVG + dj, mx, i + 1, biasn_ref, 0)
            return mx

        mx = lax.fori_loop(0, ngroups, body, mx_init())
        cm = colmax(lax.fori_loop(ngroups, (i + 1) // PVG + 1, extra, mx))
        for h in range(2):
            m0_ref[h] = cm[h]

    @pl.when(i == nb - 1)
    def _():
        def body(t, carry):
            for dj in range(PVG):
                probs_pv(last, m[last], t * PVG + dj)
            return carry

        lax.fori_loop(0, ngroups, body, 0)

    for pr in range(NPAIR):
        finalize(pr)


def _moba(q, k, vt, bias, sg):
    B, S, _ = q.shape
    nb = S // BLK
    return pl.pallas_call(
        _moba_kernel,
        grid=(B, nb),
        in_specs=[pl.BlockSpec((1, BLK, MW), lambda b, i: (b, i, 0)),
                  pl.BlockSpec((1, S, MW), lambda b, i: (b, 0, 0)),
                  pl.BlockSpec((1, nb, MW, BLK), lambda b, i: (b, 0, 0, 0)),
                  pl.BlockSpec((1, MH, nb, BLK), lambda b, i: (b, 0, 0, i)),
                  pl.BlockSpec((1, BLK, MW), lambda b, i: (b, i, 0)),
                  pl.BlockSpec((1, BLK, LANES), lambda b, i: (b, jnp.minimum(i + 1, nb - 1), 0)),
                  pl.BlockSpec((1, 2, nb, BLK), lambda b, i: (b, 0, 0, jnp.minimum(i + 1, nb - 1)))],
        out_specs=pl.BlockSpec((1, BLK, MW), lambda b, i: (b, i, 0)),
        out_shape=jax.ShapeDtypeStruct((B, S, MW), BF16),
        scratch_shapes=[pltpu.VMEM((3, BLK, BLK), F32),
                        pltpu.VMEM((2, nb, BLK, BLK), F32),
                        pltpu.VMEM((2, nb, BLK, BLK), F32),
                        pltpu.VMEM((2, 2, BLK, LANES), BF16),
                        pltpu.VMEM((NPAIR, 2, MD + ONES, BLK), F32),
                        pltpu.VMEM((2, 1, BLK), F32)],
        compiler_params=pltpu.CompilerParams(
            dimension_semantics=("parallel", "arbitrary"), vmem_limit_bytes=VMEM_LIMIT),
        name="moba",
    )(q, k, vt, bias, sg, q, bias)


def _dn_kernel(x_ref, sz_ref, gb_ref, gbt_ref, cw_ref, nw_ref, o_ref, xbuf, qkv, state, mqs, cs, os_, gls):
    t = pl.program_id(1)
    halo = 8

    @pl.when(t == 0)
    def _():
        xbuf[0:halo, :] = jnp.zeros((halo, 3 * DW), F32)
        state[...] = jnp.zeros_like(state)

    @pl.when(t > 0)
    def _():
        xbuf[0:halo, :] = xbuf[TD:TD + halo, :]

    xbuf[halo:halo + TD, :] = x_ref[0]

    for n in range(3 * DH):
        sl = slice(n * DK, (n + 1) * DK)
        x0 = xbuf[:, sl]
        x1 = pltpu.roll(x0, 1, axis=0)
        a = cw_ref[3:4, sl] * x0 + cw_ref[2:3, sl] * x1
        b = cw_ref[1:2, sl] * x0 + cw_ref[0:1, sl] * x1
        y = (a + pltpu.roll(b, 2, axis=0))[halo:halo + TD]
        y = _silu(y)
        if n < 2 * DH:
            y = y * lax.rsqrt(jnp.sum(y * y, axis=-1, keepdims=True) + EPS)
        if n < DH:
            y = y * (DK ** -0.5)
        qkv[:, sl] = y

    r = lax.broadcasted_iota(jnp.int32, (CH, CH), 0)
    c = lax.broadcasted_iota(jnp.int32, (CH, CH), 1)
    tril = r >= c
    r2 = lax.broadcasted_iota(jnp.int32, (CH, 2 * CH), 0)
    c2 = lax.broadcasted_iota(jnp.int32, (CH, 2 * CH), 1)
    c4 = lax.broadcasted_iota(jnp.int32, (CH, 2 * DK), 1)
    keep = jnp.logical_or(jnp.logical_and(c2 < CH, c2 >= r2), c2 - CH > r2)
    ones_l3 = jnp.concatenate([tril.astype(F32)] * 3, axis=1).astype(BF16)
    ones_u3 = jnp.concatenate([(r <= c).astype(F32)] * 3, axis=0).astype(BF16)
    nw = nw_ref[...]

    def local(it, _):
        units = [(dc, h) for dc in range(CU) for h in range(DH)]
        ci = [it * CU + dc for dc in range(CU)]
        r0 = [pl.multiple_of(c_ * CH, CH) for c_ in ci]
        gcol = [gb_ref[0, pl.ds(r0[dc], CH), :] for dc in range(CU)]
        grow = [gbt_ref[0, ci[dc]] for dc in range(CU)]
        gc = [jnp.dot(ones_l3, jnp.concatenate(_split3_bf16(g), axis=0).astype(BF16),
                      preferred_element_type=F32) for g in gcol]
        gr = [jnp.dot(jnp.concatenate(_split3_bf16(g), axis=1).astype(BF16), ones_u3,
                      preferred_element_type=F32) for g in grow]
        q = [qkv[pl.ds(r0[dc], CH), h * DK:(h + 1) * DK] for dc, h in units]
        k = [qkv[pl.ds(r0[dc], CH), DW + h * DK:DW + (h + 1) * DK] for dc, h in units]
        v = [qkv[pl.ds(r0[dc], CH), 2 * DW + h * DK:2 * DW + (h + 1) * DK] for dc, h in units]
        beta = [gcol[dc][:, h:h + 1] for dc, h in units]
        gcum = [gc[dc][:, DH + h:DH + h + 1] for dc, h in units]
        glast = [g[CH - 1:CH, :] for g in gcum]
        us = range(len(units))
        eg = [jnp.exp(g) for g in gcum]
        kb = [k[u] * beta[u] for u in us]
        dT = []
        for u, (dc, h) in enumerate(units):
            g2 = jnp.concatenate([gr[dc][DH + h:DH + h + 1, :]] * 2, axis=1)
            dT.append(jnp.where(keep, jnp.exp(jnp.where(keep, g2 - gcum[u], 0.0)), 0.0))
        aqt = [lax.dot_general(k[u].astype(BF16), jnp.concatenate([q[u], kb[u]], axis=0).astype(BF16), NT,
                               preferred_element_type=F32) for u in us]
        w = [jnp.concatenate([k[u] * jnp.exp(glast[u] - gcum[u]), aqt[u] * dT[u]], axis=1) for u in us]
        for lvl in range(6):
            for u in us:
                wh, wlo = _split_bf16(w[u])
                ph = pltpu.roll(wh[:, DK:], CH, axis=1)
                lhs = jnp.concatenate([jnp.where(c2 < CH, ph, wlo[:, DK:]), ph[:, :CH]], axis=1)
                res = jnp.dot(lhs.astype(BF16), jnp.concatenate([wh, wh, wlo], axis=0).astype(BF16),
                              preferred_element_type=F32)
                w[u] = jnp.where(c4 < 3 * CH, w[u] - res if lvl == 0 else w[u] + res, res)
        x0 = [jnp.concatenate([v[u] * beta[u], kb[u] * eg[u]], axis=1).astype(BF16) for u in us]
        rr = [jnp.dot(w[u][:, :3 * CH].T.astype(BF16), x0[u], preferred_element_type=F32) for u in us]
        for dc in range(CU):
            gls[ci[dc]] = jnp.exp(gc[dc][CH - 1:CH, :])
        for u, (dc, h) in enumerate(units):
            mqs[h, ci[dc], 0:DK, :] = rr[u][0:DK, DK:].astype(BF16)
            mqs[h, ci[dc], DK:DK + CH, :] = (q[u] * eg[u] - rr[u][DK:DK + CH, DK:]).astype(BF16)
            cs[h, ci[dc]] = rr[u][0:DK, :DK]
            os_[pl.ds(r0[dc], CH), h * DK:(h + 1) * DK] = rr[u][DK:DK + CH, :DK]
        return 0

    lax.fori_loop(0, TD // (CH * CU), local, 0)

    for ci in range(TD // CH):
        r0 = ci * CH
        gl_all = gls[ci]
        for h in range(DH):
            st = state[h]
            rr = jnp.dot(mqs[h, ci], st.astype(BF16), preferred_element_type=F32)
            state[h] = st * gl_all[:, DH + h:DH + h + 1] - rr[:DK] + cs[h, ci]
            o = rr[DK:] + os_[r0:r0 + CH, h * DK:(h + 1) * DK]
            o = o * lax.rsqrt(jnp.mean(o * o, axis=-1, keepdims=True) + EPS) * nw
            o = o * sz_ref[0, r0:r0 + CH, h * DK:(h + 1) * DK]
            o_ref[0, r0:r0 + CH, h * DK:(h + 1) * DK] = o.astype(o_ref.dtype)


def _deltanet(dx, sz, gb, gbt, conv_w, dn_norm_w):
    B, S, _ = dx.shape
    nt = S // TD
    return pl.pallas_call(
        _dn_kernel,
        grid=(B, nt),
        in_specs=[pl.BlockSpec((1, TD, 3 * DW), lambda b, t: (b, t, 0)),
                  pl.BlockSpec((1, TD, DW), lambda b, t: (b, t, 0)),
                  pl.BlockSpec((1, TD, LANES), lambda b, t: (b, t, 0)),
                  pl.BlockSpec((1, TD // CH, 8, CH), lambda b, t: (b, t, 0, 0)),
                  pl.BlockSpec((CONV, 3 * DW), lambda b, t: (0, 0)),
                  pl.BlockSpec((1, DK), lambda b, t: (0, 0))],
        out_specs=pl.BlockSpec((1, TD, DW), lambda b, t: (b, t, 0)),
        out_shape=jax.ShapeDtypeStruct((B, S, DW), BF16),
        scratch_shapes=[pltpu.VMEM((TD + 8, 3 * DW), F32),
                        pltpu.VMEM((TD, 3 * DW), F32),
                        pltpu.VMEM((DH, DK, DK), F32),
                        pltpu.VMEM((DH, TD // CH, DK + CH, DK), BF16),
                        pltpu.VMEM((DH, TD // CH, DK, DK), F32),
                        pltpu.VMEM((TD, DW), F32),
                        pltpu.VMEM((TD // CH, 1, LANES), F32)],
        compiler_params=pltpu.CompilerParams(
            dimension_semantics=("parallel", "arbitrary"), vmem_limit_bytes=VMEM_LIMIT),
        name="deltanet",
    )(dx, sz, gb, gbt, conv_w, dn_norm_w.reshape(1, DK))


def _outproj_kernel(om_ref, od_ref, x_ref, mod_ref, pw_ref, wt_ref, wb_ref, o_ref):
    y = jnp.dot(om_ref[0], wt_ref[...], preferred_element_type=F32)
    y = y + jnp.dot(od_ref[0], wb_ref[...], preferred_element_type=F32)
    yn = y * lax.rsqrt(jnp.mean(y * y, axis=-1, keepdims=True) + EPS) * pw_ref[...]
    o_ref[0] = x_ref[0] + mod_ref[0, 2:3, :] * yn


def _outproj(om, od, x, mod3, post_w, w_out):
    B, S, _ = x.shape
    wb = w_out.astype(BF16)
    row = lambda w: pl.BlockSpec((1, TO, w), lambda b, t: (b, t, 0))
    return pl.pallas_call(
        _outproj_kernel,
        grid=(B, S // TO),
        in_specs=[row(MW), row(DW), row(D),
                  pl.BlockSpec((1, 3, D), lambda b, t: (b, 0, 0)),
                  pl.BlockSpec((1, D), lambda b, t: (0, 0)),
                  pl.BlockSpec((MW, D), lambda b, t: (0, 0)),
                  pl.BlockSpec((DW, D), lambda b, t: (0, 0))],
        out_specs=row(D),
        out_shape=jax.ShapeDtypeStruct((B, S, D), F32),
        compiler_params=pltpu.CompilerParams(
            dimension_semantics=("parallel", "parallel"), vmem_limit_bytes=VMEM_LIMIT),
        name="outproj",
    )(om, od, x, mod3, post_w.reshape(1, D), wb[:MW], wb[MW:])


def kernel(x, c, ada_w, ada_b, pre_norm_w, post_norm_w, w_in, conv_w, a_log, dt_bias, dn_norm_w, w_out):
    B, S, _ = x.shape
    assert S % TD == 0 and S % TO == 0 and S % (BLK * PVG) == 0 and S % TM == 0 and TM % BLK == 0
    depth = ada_w.shape[0]
    for l in range(depth):
        mod3 = _mod(c, ada_w[l], ada_b[l]).reshape(B, 3, D)
        q, k, bias, vt, sg, dx, sz, gb, gbt = _inproj(x, mod3, pre_norm_w[l], w_in[l], a_log[l], dt_bias[l])
        om = _moba(q, k, vt, bias, sg)
        od = _deltanet(dx, sz, gb, gbt, conv_w[l], dn_norm_w[l])
        x = _outproj(om, od, x, mod3, post_norm_w[l], w_out[l])
    return x
```
